```python
import jax, jax.numpy as jnp
from jax import lax
import numpy as np

D_MODEL = 1024
BATCH = 4
SEQ = 8192
DEPTH = 2

HEAD_DIM = 64
DSW_GROUPS = ((128, 1), (512, 4), (2048, 16))
DSW_HEADS_PER_GROUP = 4
DSW_HEADS = DSW_HEADS_PER_GROUP * len(DSW_GROUPS)
SB_HEADS = D_MODEL // (4 * HEAD_DIM)
W_A = DSW_HEADS * HEAD_DIM
W_B = SB_HEADS * HEAD_DIM
OUT_A = DSW_HEADS_PER_GROUP * HEAD_DIM
OUT_B = W_B
N_IN = 3 * W_A + 3 * W_B + 2 * D_MODEL
BLOCK = 128
D_FF = 2816
CONV_WIDTH = 3
RMS_EPS = 1e-6

kernel_name = "hybrid_dilated_stickbreaking_gated_block"


def rmsnorm(x, g):
    xf = x.astype(jnp.float32)
    y = xf * lax.rsqrt(jnp.mean(xf * xf, axis=-1, keepdims=True) + RMS_EPS)
    return (y * g.astype(jnp.float32)).astype(x.dtype)


def alibi_slopes(n):
    return jnp.asarray(2.0 ** (-8.0 * np.arange(1, n + 1) / n), dtype=jnp.float32)


def dilated_window_attention(q, k, v, slopes, window, dilation):
    B, S, H, hd = q.shape
    W = window // dilation
    L = S // dilation
    nb = -(-L // W)
    Lp = nb * W

    def to_sub(a):
        a = a.reshape(B, L, dilation, H, hd).transpose(0, 2, 3, 1, 4)
        return jnp.pad(a, ((0, 0), (0, 0), (0, 0), (0, Lp - L), (0, 0)))

    def windows(a):
        ap = jnp.pad(to_sub(a), ((0, 0), (0, 0), (0, 0), (W, 0), (0, 0))).reshape(B, dilation, H, nb + 1, W, hd)
        return jnp.concatenate([ap[:, :, :, :-1], ap[:, :, :, 1:]], axis=-2)

    qs = to_sub(q).reshape(B, dilation, H, nb, W, hd)
    kw = windows(k)
    vw = windows(v).astype(jnp.float32)
    scores = jnp.einsum('brhnqc,brhnkc->brhnqk', qs, kw).astype(jnp.float32) * (hd ** -0.5)
    qi = jnp.arange(W)[:, None]
    kj = jnp.arange(2 * W)[None, :]
    delta = qi + W - kj
    key_sub = (jnp.arange(nb) * W - W)[:, None, None] + kj[None]
    valid = (delta >= 0) & (delta <= W) & (key_sub >= 0)
    bias = -slopes[:, None, None, None] * (delta * dilation).astype(jnp.float32)
    scores = jnp.where(valid, scores + bias, -jnp.inf)
    mx = jnp.max(scores, axis=-1, keepdims=True)
    p = jnp.exp(scores - mx)
    den = jnp.sum(p, axis=-1, keepdims=True)
    o = jnp.einsum('brhnqk,brhnkc->brhnqc', p, vw) / den
    lse = (mx + jnp.log(den))[..., 0]
    o = o.reshape(B, dilation, H, Lp, hd)[:, :, :, :L].transpose(0, 3, 1, 2, 4).reshape(B, S, H, hd)
    lse = lse.reshape(B, dilation, H, Lp)[:, :, :, :L].transpose(0, 3, 1, 2).reshape(B, S, H)
    return o, lse


def dilated_mixture(q, k, v):
    B, S = q.shape[:2]
    slopes = alibi_slopes(DSW_HEADS)
    outs, lses = [], []
    for g, (window, dilation) in enumerate(DSW_GROUPS):
        hs = slice(g * DSW_HEADS_PER_GROUP, (g + 1) * DSW_HEADS_PER_GROUP)
        o, lse = dilated_window_attention(q[:, :, hs], k[:, :, hs], v[:, :, hs], slopes[hs], window, dilation)
        outs.append(o)
        lses.append(lse)
    alpha = jax.nn.softmax(jnp.stack(lses, axis=0), axis=0)
    o = jnp.sum(alpha[..., None] * jnp.stack(outs, axis=0), axis=0)
    return o.reshape(B, S, OUT_A).astype(q.dtype)


def stick_breaking_attention(q, k, v):
    B, S, H, hd = q.shape
    nq = S // BLOCK
    scale = hd ** -0.5
    qb = q.reshape(B, nq, BLOCK, H, hd).transpose(1, 0, 3, 2, 4)
    kt = k.transpose(0, 2, 1, 3)
    vt = v.transpose(0, 2, 1, 3).astype(jnp.float32)
    key_pos = jnp.arange(S)

    def block(args):
        q_blk, i = args
        z = jnp.einsum('bhqc,bhkc->bhqk', q_blk, kt).astype(jnp.float32) * scale
        qpos = i * BLOCK + jnp.arange(BLOCK)
        causal = key_pos[None, :] < qpos[:, None]
        log_stay = jnp.where(causal, jax.nn.log_sigmoid(-z), 0.0)
        later = lax.cumsum(log_stay, axis=3, reverse=True) - log_stay
        attn = jnp.where(causal, jnp.exp(jax.nn.log_sigmoid(z) + later), 0.0)
        return jnp.einsum('bhqk,bhkc->bhqc', attn, vt)

    o = lax.map(block, (qb, jnp.arange(nq)))
    return o.transpose(1, 0, 3, 2, 4).reshape(B, S, H * hd).astype(q.dtype)


def causal_depthwise_conv(a, w, b):
    S = a.shape[1]
    ap = jnp.pad(a, ((0, 0), (CONV_WIDTH - 1, 0), (0, 0)))
    y = b
    for i in range(CONV_WIDTH):
        y = y + ap[:, i:i + S] * w[i]
    return y


def setup_inputs(seed: int = 0) -> dict:
    key = jax.random.key(seed)
    ks = jax.random.split(key, 12)
    f32 = jnp.float32
    x = jax.random.normal(ks[0], (BATCH, SEQ, D_MODEL), f32)
    norm1 = 1.0 + 0.05 * jax.random.normal(ks[1], (DEPTH, D_MODEL), f32)
    w_in = jax.random.normal(ks[2], (DEPTH, D_MODEL, N_IN), f32) * D_MODEL ** -0.5
    b_gate = 0.01 * jax.random.normal(ks[3], (DEPTH, 2 * D_MODEL), f32)
    w_br = jax.random.normal(ks[4], (DEPTH, OUT_A + OUT_B, D_MODEL), f32) * OUT_A ** -0.5
    w_o = jax.random.normal(ks[5], (DEPTH, D_MODEL, D_MODEL), f32) * D_MODEL ** -0.5
    norm2 = 1.0 + 0.05 * jax.random.normal(ks[6], (DEPTH, D_MODEL), f32)
    w_up = jax.random.normal(ks[7], (DEPTH, D_MODEL, 2 * D_FF), f32) * D_MODEL ** -0.5
    conv_w = jax.random.normal(ks[8], (DEPTH, CONV_WIDTH, D_FF), f32) * CONV_WIDTH ** -0.5
    conv_b = 0.01 * jax.random.normal(ks[9], (DEPTH, D_FF), f32)
    w_down = jax.random.normal(ks[10], (DEPTH, D_FF, D_MODEL), f32) * D_FF ** -0.5
    norm_f = 1.0 + 0.05 * jax.random.normal(ks[11], (D_MODEL,), f32)
    return {"x": x, "norm1": norm1, "w_in": w_in, "b_gate": b_gate, "w_br": w_br, "w_o": w_o,
            "norm2": norm2, "w_up": w_up, "conv_w": conv_w, "conv_b": conv_b, "w_down": w_down,
            "norm_f": norm_f}


def reference(x, norm1, w_in, b_gate, w_br, w_o, norm2, w_up, conv_w, conv_b, w_down, norm_f):
    B, S, _ = x.shape
    splits = [W_A, 2 * W_A, 3 * W_A, 3 * W_A + W_B, 3 * W_A + 2 * W_B, 3 * W_A + 3 * W_B]
    for l in range(DEPTH):
        h = rmsnorm(x, norm1[l])
        proj = h @ w_in[l]
        qa, ka, va, qb, kb, vb, gate_pre = jnp.split(proj, splits, axis=-1)
        hd4 = lambda t, n: t.reshape(B, S, n, HEAD_DIM)
        o_a = dilated_mixture(hd4(qa, DSW_HEADS), hd4(ka, DSW_HEADS), hd4(va, DSW_HEADS))
        o_b = stick_breaking_attention(hd4(qb, SB_HEADS), hd4(kb, SB_HEADS), hd4(vb, SB_HEADS))
        gates = jax.nn.sigmoid(gate_pre + b_gate[l])
        g_a, g_b = gates[..., :D_MODEL], gates[..., D_MODEL:]
        merged = g_a * (o_a @ w_br[l, :OUT_A]) + g_b * (o_b @ w_br[l, OUT_A:])
        x = x + merged @ w_o[l]
        h2 = rmsnorm(x, norm2[l])
        up = h2 @ w_up[l]
        a, v = up[..., :D_FF], up[..., D_FF:]
        a = causal_depthwise_conv(a, conv_w[l], conv_b[l])
        x = x + (jax.nn.gelu(a, approximate=False) * v) @ w_down[l]
    return rmsnorm(x, norm_f)
```

```python
import functools

import numpy as np
import jax
import jax.numpy as jnp
from jax import lax
from jax.experimental import pallas as pl
from jax.experimental.pallas import tpu as pltpu

D_MODEL = 1024
HEAD_DIM = 64
DSW_GROUPS = ((128, 1), (512, 4), (2048, 16))
N_GROUPS = len(DSW_GROUPS)
HEADS_PER_GROUP = 4
DSW_HEADS = HEADS_PER_GROUP * N_GROUPS
SB_HEADS = 4
W_A = DSW_HEADS * HEAD_DIM
W_B = SB_HEADS * HEAD_DIM
OUT_A = HEADS_PER_GROUP * HEAD_DIM
N_IN = 3 * W_A + 3 * W_B + 2 * D_MODEL
D_FF = 2816
CONV_WIDTH = 3
RMS_EPS = 1e-6
QK_SCALE = HEAD_DIM ** -0.5

LANES = 128
WIN = 128
MASKED = -1e30

VMEM_LIMIT = 56 * 1024 * 1024

F32 = jnp.float32
BF16 = jnp.bfloat16


def _dot(a, b):
    return jnp.dot(a, b, preferred_element_type=F32)


def _dot_nt(a, b):
    return lax.dot_general(a, b, (((1,), (1,)), ((), ())), preferred_element_type=F32)


def _rms(x, g):
    ms = jnp.mean(x * x, axis=-1, keepdims=True)
    return x * lax.rsqrt(ms + RMS_EPS) * g


IN_TM = 512
IN_CHUNK = 256


def _in_proj_kernel(x_ref, g_ref, w_ref, pa_ref, pb_ref, pg_ref):
    h = _rms(x_ref[...], g_ref[...]).astype(BF16)
    outs = ((pa_ref, 0, 3 * W_A), (pb_ref, 3 * W_A, 3 * W_B), (pg_ref, 3 * W_A + 3 * W_B, 2 * D_MODEL))
    for ref, base, width in outs:
        for c in range(0, width, IN_CHUNK):
            ref[:, c:c + IN_CHUNK] = _dot(h, w_ref[:, base + c:base + c + IN_CHUNK]).astype(ref.dtype)


def _in_proj(x, g, w):
    t = x.shape[0]
    return pl.pallas_call(
        _in_proj_kernel,
        grid=(t // IN_TM,),
        in_specs=[
            pl.BlockSpec((IN_TM, D_MODEL), lambda i: (i, 0)),
            pl.BlockSpec((1, D_MODEL), lambda i: (0, 0)),
            pl.BlockSpec((D_MODEL, N_IN), lambda i: (0, 0)),
        ],
        out_specs=[
            pl.BlockSpec((IN_TM, 3 * W_A), lambda i: (i, 0)),
            pl.BlockSpec((IN_TM, 3 * W_B), lambda i: (i, 0)),
            pl.BlockSpec((IN_TM, 2 * D_MODEL), lambda i: (i, 0)),
        ],
        out_shape=[
            jax.ShapeDtypeStruct((t, 3 * W_A), F32),
            jax.ShapeDtypeStruct((t, 3 * W_B), BF16),
            jax.ShapeDtypeStruct((t, 2 * D_MODEL), BF16),
        ],
        compiler_params=pltpu.CompilerParams(dimension_semantics=("arbitrary",), vmem_limit_bytes=VMEM_LIMIT),
        name="in_proj",
    )(x, g.reshape(1, D_MODEL), w)


A_TILE = WIN * max(d for _, d in DSW_GROUPS)


def _alibi_bias():
    slopes = 2.0 ** (-8.0 * np.arange(1, DSW_HEADS + 1) / DSW_HEADS)
    qi = np.arange(WIN)[:, None]
    kj = np.arange(2 * WIN)[None, :]
    delta = qi + WIN - kj
    valid = (delta >= 0) & (delta <= WIN)
    out = np.empty((N_GROUPS, HEADS_PER_GROUP, 2, WIN, 2 * WIN), np.float32)
    for g, (_, dil) in enumerate(DSW_GROUPS):
        for h in range(HEADS_PER_GROUP):
            b = -slopes[g * HEADS_PER_GROUP + h] * (delta * dil).astype(np.float32)
            out[g, h, 0] = np.where(valid, b, MASKED)
            out[g, h, 1] = np.where(valid & (kj >= WIN), b, MASKED)
    return out


def _mixer_a_kernel(*refs):
    q_refs = refs[0:3]
    kc_refs = refs[3:6]
    vc_refs = refs[6:9]
    kp_refs = refs[9:12]
    vp_refs = refs[12:15]
    bias_ref = refs[15]
    o_ref = refs[16]
    acc_s, m_s, l_s = refs[17:20]

    first = (pl.program_id(1) == 0).astype(jnp.int32)
    lane = lax.broadcasted_iota(jnp.int32, (WIN, LANES), 1)
    head0 = lane < HEAD_DIM

    for g, (_, dil) in enumerate(DSW_GROUPS):
        q_ref, kc_ref, vc_ref = q_refs[g], kc_refs[g], vc_refs[g]

        def rows(ref, start, dil=dil):
            if dil == 1:
                return ref[pl.ds(start, WIN), :]
            return ref[pl.ds(start, WIN, stride=dil), :]

        def unit(kprev_ref, vprev_ref, prev_off, off, sel, g=g, dil=dil, q_ref=q_ref, kc_ref=kc_ref,
                 vc_ref=vc_ref, rows=rows):
            q = rows(q_ref, off) * QK_SCALE
            k_p = rows(kprev_ref, prev_off).astype(BF16)
            k_c = rows(kc_ref, off).astype(BF16)
            v_p = rows(vprev_ref, prev_off).astype(BF16)
            v_c = rows(vc_ref, off).astype(BF16)
            per_head = []
            for h in range(2):
                qh = jnp.where(head0 if h == 0 else jnp.logical_not(head0), q, 0.0).astype(BF16)
                s_p = _dot_nt(qh, k_p) + bias_ref[g, h, sel, :, 0:WIN]
                s_c = _dot_nt(qh, k_c) + bias_ref[g, h, sel, :, WIN:2 * WIN]
                m = jnp.maximum(jnp.max(s_p, axis=-1, keepdims=True), jnp.max(s_c, axis=-1, keepdims=True))
                p_p = jnp.exp(s_p - m)
                p_c = jnp.exp(s_c - m)
                l = jnp.sum(p_p, axis=-1, keepdims=True) + jnp.sum(p_c, axis=-1, keepdims=True)
                pv = _dot(p_p.astype(BF16), v_p) + _dot(p_c.astype(BF16), v_c)
                per_head.append((m, l, pv))
            (m0, l0, pv0), (m1, l1, pv1) = per_head
            dst = pl.ds(off, WIN) if dil == 1 else pl.ds(off, WIN, stride=dil)
            acc_s[g, dst, :] = jnp.where(head0, pv0, pv1)
            m_s[g, dst, :] = jnp.where(head0, m0, m1)
            l_s[g, dst, :] = jnp.where(head0, l0, l1)

        def head_body(r, c, unit=unit, kp_ref=kp_refs[g], vp_ref=vp_refs[g]):
            unit(kp_ref, vp_ref, r, r, first)
            return c

        lax.fori_loop(0, dil, head_body, 0)

        n_units = A_TILE // WIN
        if n_units > dil:
            shift = dil.bit_length() - 1

            def tail_body(idx, c, unit=unit, kc_ref=kc_ref, vc_ref=vc_ref, dil=dil, shift=shift):
                u = lax.shift_right_logical(idx, shift)
                r = jnp.bitwise_and(idx, dil - 1)
                off = u * (WIN * dil) + r
                unit(kc_ref, vc_ref, off - WIN * dil, off, 0)
                return c

            lax.fori_loop(dil, n_units, tail_body, 0)

    m_all = jnp.maximum(jnp.maximum(m_s[0], m_s[1]), m_s[2])
    num = jnp.zeros((A_TILE, LANES), F32)
    den = jnp.zeros((A_TILE, LANES), F32)
    for g in range(N_GROUPS):
        w = jnp.exp(m_s[g] - m_all)
        num = num + w * acc_s[g]
        den = den + w * l_s[g]
    o_ref[...] = (num / den).astype(o_ref.dtype)


def _mixer_a(pa, bias, batch, seq):
    t = pa.shape[0]
    tiles = seq // A_TILE
    qcol = lambda g: (lambda b, n, hp, g=g: (b * tiles + n, 0 * 6 + 2 * g + hp))
    kcol = lambda g: (lambda b, n, hp, g=g: (b * tiles + n, 1 * 6 + 2 * g + hp))
    vcol = lambda g: (lambda b, n, hp, g=g: (b * tiles + n, 2 * 6 + 2 * g + hp))

    def prev(kind, g):
        per_tile = A_TILE // (WIN * DSW_GROUPS[g][1])
        return lambda b, n, hp: (jnp.maximum((b * tiles + n) * per_tile - 1, 0), kind * 6 + 2 * g + hp)

    in_specs = (
        [pl.BlockSpec((A_TILE, LANES), qcol(g)) for g in range(N_GROUPS)]
        + [pl.BlockSpec((A_TILE, LANES), kcol(g)) for g in range(N_GROUPS)]
        + [pl.BlockSpec((A_TILE, LANES), vcol(g)) for g in range(N_GROUPS)]
        + [pl.BlockSpec((WIN * DSW_GROUPS[g][1], LANES), prev(1, g)) for g in range(N_GROUPS)]
        + [pl.BlockSpec((WIN * DSW_GROUPS[g][1], LANES), prev(2, g)) for g in range(N_GROUPS)]
        + [pl.BlockSpec((N_GROUPS, 2, 2, WIN, 2 * WIN), lambda b, n, hp: (0, hp, 0, 0, 0))]
    )
    return pl.pallas_call(
        _mixer_a_kernel,
        grid=(batch, tiles, 2),
        in_specs=in_specs,
        out_specs=pl.BlockSpec((A_TILE, LANES), lambda b, n, hp: (b * tiles + n, hp)),
        out_shape=jax.ShapeDtypeStruct((t, OUT_A), BF16),
        scratch_shapes=[pltpu.VMEM((N_GROUPS, A_TILE, LANES), F32)] * 3,
        compiler_params=pltpu.CompilerParams(
            dimension_semantics=("arbitrary", "arbitrary", "arbitrary"), vmem_limit_bytes=VMEM_LIMIT),
        name="mixer_a",
    )(*([pa] * 15), bias)


B_BLK = 128


def _suffix_sum_matrix():
    j = np.arange(B_BLK)[:, None]
    s = np.arange(B_BLK)[None, :]
    return np.concatenate([(j > s).astype(np.float32), np.ones((B_BLK, B_BLK), np.float32)], axis=1)


def _mixer_b_kernel(q_ref, k_ref, v_ref, tri_ref, o_ref, acc_s, run_s):
    i = pl.program_id(1)
    lane = lax.broadcasted_iota(jnp.int32, (B_BLK, W_B), 1)
    q = q_ref[...] * QK_SCALE
    q_heads = [jnp.where((lane >= h * HEAD_DIM) & (lane < (h + 1) * HEAD_DIM), q, 0.0).astype(BF16)
               for h in range(SB_HEADS)]
    tri = tri_ref[...]
    t_loc = lax.broadcasted_iota(jnp.int32, (B_BLK, B_BLK), 0)
    s_loc = lax.broadcasted_iota(jnp.int32, (B_BLK, B_BLK), 1)
    causal = s_loc < t_loc

    acc_s[...] = jnp.zeros_like(acc_s)
    run_s[...] = jnp.zeros_like(run_s)

    def block(j, diagonal):
        start = pl.multiple_of(j * B_BLK, B_BLK)
        k = k_ref[pl.ds(start, B_BLK), :]
        v = v_ref[pl.ds(start, B_BLK), :]
        for h in range(SB_HEADS):
            z = _dot_nt(q_heads[h], k)
            soft = jnp.log1p(jnp.exp(-jnp.abs(z)))
            log_stay = -(jnp.maximum(z, 0.0) + soft)
            log_take = jnp.minimum(z, 0.0) - soft
            if diagonal:
                log_stay = jnp.where(causal, log_stay, 0.0)
            hi = log_stay.astype(BF16)
            lo = (log_stay - hi.astype(F32)).astype(BF16)
            sums = _dot(hi, tri) + _dot(lo, tri)
            run = run_s[h]
            p = jnp.exp(log_take + sums[:, 0:B_BLK] + run)
            if diagonal:
                p = jnp.where(causal, p, 0.0)
            run_s[h] = run + sums[:, B_BLK:2 * B_BLK]
            acc_s[h] += _dot(p.astype(BF16), v)

    block(i, True)

    def body(jj, c):
        block(i - 1 - jj, False)
        return c

    lax.fori_loop(0, i, body, 0)

    out = acc_s[SB_HEADS - 1]
    for h in range(SB_HEADS - 2, -1, -1):
        out = jnp.where(lane < (h + 1) * HEAD_DIM, acc_s[h], out)
    o_ref[...] = out.astype(o_ref.dtype)


def _mixer_b(pb, tri, batch, seq):
    t = pb.shape[0]
    nq = seq // B_BLK
    pb3 = pb.reshape(batch, seq, 3 * W_B)
    out = pl.pallas_call(
        _mixer_b_kernel,
        grid=(batch, nq),
        in_specs=[
            pl.BlockSpec((None, B_BLK, W_B), lambda b, i: (b, i, 0)),
            pl.BlockSpec((None, seq, W_B), lambda b, i: (b, 0, 1)),
            pl.BlockSpec((None, seq, W_B), lambda b, i: (b, 0, 2)),
            pl.BlockSpec((B_BLK, 2 * B_BLK), lambda b, i: (0, 0)),
        ],
        out_specs=pl.BlockSpec((None, B_BLK, W_B), lambda b, i: (b, i, 0)),
        out_shape=jax.ShapeDtypeStruct((batch, seq, W_B), BF16),
        scratch_shapes=[pltpu.VMEM((SB_HEADS, B_BLK, W_B), F32), pltpu.VMEM((SB_HEADS, B_BLK, B_BLK), F32)],
        compiler_params=pltpu.CompilerParams(
            dimension_semantics=("arbitrary", "arbitrary"), vmem_limit_bytes=VMEM_LIMIT),
        name="mixer_b",
    )(pb3, pb3, pb3, tri)
    return out.reshape(t, W_B)


MERGE_TM = 512


def _merge_kernel(x_ref, oa_ref, ob_ref, pg_ref, bg_ref, wa_ref, wb_ref, wo_ref, o_ref):
    gates = jax.nn.sigmoid(pg_ref[...].astype(F32) + bg_ref[...])
    ya = _dot(oa_ref[...], wa_ref[...])
    yb = _dot(ob_ref[...], wb_ref[...])
    merged = gates[:, 0:D_MODEL] * ya + gates[:, D_MODEL:2 * D_MODEL] * yb
    o_ref[...] = x_ref[...] + _dot(merged.astype(BF16), wo_ref[...])


def _merge(x, oa, ob, pg, bg, wa, wb, wo):
    t = x.shape[0]
    row = lambda w: pl.BlockSpec((MERGE_TM, w), lambda i: (i, 0))
    full = lambda r, c: pl.BlockSpec((r, c), lambda i: (0, 0))
    return pl.pallas_call(
        _merge_kernel,
        grid=(t // MERGE_TM,),
        in_specs=[row(D_MODEL), row(OUT_A), row(W_B), row(2 * D_MODEL), full(1, 2 * D_MODEL),
                  full(OUT_A, D_MODEL), full(W_B, D_MODEL), full(D_MODEL, D_MODEL)],
        out_specs=row(D_MODEL),
        out_shape=jax.ShapeDtypeStruct((t, D_MODEL), F32),
        compiler_params=pltpu.CompilerParams(dimension_semantics=("arbitrary",), vmem_limit_bytes=VMEM_LIMIT),
        name="merge",
    )(x, oa, ob, pg, bg.reshape(1, 2 * D_MODEL), wa, wb, wo)


FFN_TM = 1024
FFN_TF = 256
HALO = 16


def _ffn_kernel(x_ref, xh_ref, g_ref, wg_ref, wv_ref, cw_ref, cb_ref, wd_ref, gf_ref, o_ref,
                h_s, up_s, acc_s, *, tiles_per_seq, final_norm):
    i = pl.program_id(0)
    f = pl.program_id(1)

    @pl.when(f == 0)
    def _():
        g = g_ref[...]
        h_s[0:HALO, :] = _rms(xh_ref[...], g).astype(BF16)
        h_s[HALO:HALO + FFN_TM, :] = _rms(x_ref[...], g).astype(BF16)
        acc_s[...] = jnp.zeros_like(acc_s)

    up = _dot(h_s[...], wg_ref[...])
    seq_start = (i % tiles_per_seq) == 0
    up_s[0:HALO, :] = jnp.where(seq_start, 0.0, up[0:HALO])
    up_s[HALO:HALO + FFN_TM, :] = up[HALO:HALO + FFN_TM]
    a = cb_ref[...]
    for tap in range(CONV_WIDTH):
        lag = CONV_WIDTH - 1 - tap
        a = a + up_s[HALO - lag:HALO - lag + FFN_TM, :] * cw_ref[tap:tap + 1, :]
    val = _dot(h_s[HALO:HALO + FFN_TM, :], wv_ref[...])
    gelu = 0.5 * a * (1.0 + lax.erf(a * (2.0 ** -0.5)))
    act = (gelu * val).astype(BF16)
    acc_s[...] += _dot(act, wd_ref[...])

    @pl.when(f == pl.num_programs(1) - 1)
    def _():
        y = x_ref[...] + acc_s[...]
        if final_norm:
            y = _rms(y, gf_ref[...])
        o_ref[...] = y


def _ffn(x, g, w_up, cw, cb, w_down, gf, seq, final_norm):
    t = x.shape[0]
    nf = D_FF // FFN_TF
    halo_blocks = FFN_TM // HALO
    kern = functools.partial(_ffn_kernel, tiles_per_seq=seq // FFN_TM, final_norm=final_norm)
    return pl.pallas_call(
        kern,
        grid=(t // FFN_TM, nf),
        in_specs=[
            pl.BlockSpec((FFN_TM, D_MODEL), lambda i, f: (i, 0)),
            pl.BlockSpec((HALO, D_MODEL), lambda i, f: (jnp.maximum(i * halo_blocks - 1, 0), 0)),
            pl.BlockSpec((1, D_MODEL), lambda i, f: (0, 0)),
            pl.BlockSpec((D_MODEL, FFN_TF), lambda i, f: (0, f)),
            pl.BlockSpec((D_MODEL, FFN_TF), lambda i, f: (0, nf + f)),
            pl.BlockSpec((CONV_WIDTH, FFN_TF), lambda i, f: (0, f)),
            pl.BlockSpec((1, FFN_TF), lambda i, f: (0, f)),
            pl.BlockSpec((FFN_TF, D_MODEL), lambda i, f: (f, 0)),
            pl.BlockSpec((1, D_MODEL), lambda i, f: (0, 0)),
        ],
        out_specs=pl.BlockSpec((FFN_TM, D_MODEL), lambda i, f: (i, 0)),
        out_shape=jax.ShapeDtypeStruct((t, D_MODEL), F32),
        scratch_shapes=[
            pltpu.VMEM((HALO + FFN_TM, D_MODEL), BF16),
            pltpu.VMEM((HALO + FFN_TM, FFN_TF), F32),
            pltpu.VMEM((FFN_TM, D_MODEL), F32),
        ],
        compiler_params=pltpu.CompilerParams(
            dimension_semantics=("arbitrary", "arbitrary"), vmem_limit_bytes=VMEM_LIMIT),
        name="ffn",
    )(x, x, g.reshape(1, D_MODEL), w_up, w_up, cw, cb.reshape(1, D_FF), w_down, gf.reshape(1, D_MODEL))


def kernel(x, norm1, w_in, b_gate, w_br, w_o, norm2, w_up, conv_w, conv_b, w_down, norm_f):
    batch, seq, d = x.shape
    depth = norm1.shape[0]
    assert d == D_MODEL and seq % A_TILE == 0 and seq % FFN_TM == 0
    t = batch * seq
    bias = jnp.asarray(_alibi_bias())
    tri = jnp.asarray(_suffix_sum_matrix(), dtype=BF16)
    xf = x.reshape(t, d)
    for l in range(depth):
        pa, pb, pg = _in_proj(xf, norm1[l], w_in[l].astype(BF16))
        oa = _mixer_a(pa, bias, batch, seq)
        ob = _mixer_b(pb, tri, batch, seq)
        wbr = w_br[l].astype(BF16)
        x1 = _merge(xf, oa, ob, pg, b_gate[l], wbr[:OUT_A], wbr[OUT_A:], w_o[l].astype(BF16))
        xf = _ffn(x1, norm2[l], w_up[l].astype(BF16), conv_w[l], conv_b[l], w_down[l].astype(BF16),
                  norm_f, seq, final_norm=(l == depth - 1))
    return xf.reshape(batch, seq, d)
```

```python
import functools

import numpy as np
import jax
import jax.numpy as jnp
from jax import lax
from jax.experimental import pallas as pl
from jax.experimental.pallas import tpu as pltpu

D_MODEL = 1024
HEAD_DIM = 64
DSW_GROUPS = ((128, 1), (512, 4), (2048, 16))
N_GROUPS = len(DSW_GROUPS)
HEADS_PER_GROUP = 4
DSW_HEADS = HEADS_PER_GROUP * N_GROUPS
SB_HEADS = 4
W_A = DSW_HEADS * HEAD_DIM
W_B = SB_HEADS * HEAD_DIM
OUT_A = HEADS_PER_GROUP * HEAD_DIM
N_IN = 3 * W_A + 3 * W_B + 2 * D_MODEL
D_FF = 2816
CONV_WIDTH = 3
RMS_EPS = 1e-6
QK_SCALE = HEAD_DIM ** -0.5

LANES = 128
WIN = 128
MASKED = -1e30

VMEM_LIMIT = 56 * 1024 * 1024

F32 = jnp.float32
BF16 = jnp.bfloat16


def _dot(a, b):
    return jnp.dot(a, b, preferred_element_type=F32)


def _dot_nt(a, b):
    return lax.dot_general(a, b, (((1,), (1,)), ((), ())), preferred_element_type=F32)


def _rms(x, g):
    ms = jnp.mean(x * x, axis=-1, keepdims=True)
    return x * lax.rsqrt(ms + RMS_EPS) * g


IN_TM = 512
IN_CHUNK = 256
B_BLK = 128
B_TQ = 256
B_CAT = SB_HEADS * B_BLK
LOG2E = 1.4426950408889634


def _in_proj_kernel(x_ref, g_ref, w_ref, pa_ref, qb_ref, kbd_ref, vbd_ref, pg_ref):
    h = _rms(x_ref[...], g_ref[...]).astype(BF16)

    def proj(base):
        return _dot(h, w_ref[:, base:base + IN_CHUNK])

    for c in range(0, 3 * W_A, IN_CHUNK):
        pa_ref[:, c:c + IN_CHUNK] = proj(c)
    base_b = 3 * W_A
    qb_ref[...] = (proj(base_b) * (-QK_SCALE * LOG2E)).astype(BF16)
    lane = lax.broadcasted_iota(jnp.int32, (B_BLK, W_B), 1)
    for ref, base in ((kbd_ref, base_b + W_B), (vbd_ref, base_b + 2 * W_B)):
        val = proj(base)
        for r in range(IN_TM // B_BLK):
            rows = val[r * B_BLK:(r + 1) * B_BLK]
            for hd in range(SB_HEADS):
                keep = (lane >= hd * HEAD_DIM) & (lane < (hd + 1) * HEAD_DIM)
                ref[(r * SB_HEADS + hd) * B_BLK:(r * SB_HEADS + hd + 1) * B_BLK, :] = (
                    jnp.where(keep, rows, 0.0).astype(BF16))
    base_g = 3 * W_A + 3 * W_B
    for c in range(0, 2 * D_MODEL, IN_CHUNK):
        pg_ref[:, c:c + IN_CHUNK] = proj(base_g + c).astype(BF16)


def _in_proj(x, g, w):
    t = x.shape[0]
    return pl.pallas_call(
        _in_proj_kernel,
        grid=(t // IN_TM,),
        in_specs=[
            pl.BlockSpec((IN_TM, D_MODEL), lambda i: (i, 0)),
            pl.BlockSpec((1, D_MODEL), lambda i: (0, 0)),
            pl.BlockSpec((D_MODEL, N_IN), lambda i: (0, 0)),
        ],
        out_specs=[
            pl.BlockSpec((IN_TM, 3 * W_A), lambda i: (i, 0)),
            pl.BlockSpec((IN_TM, W_B), lambda i: (i, 0)),
            pl.BlockSpec((SB_HEADS * IN_TM, W_B), lambda i: (i, 0)),
            pl.BlockSpec((SB_HEADS * IN_TM, W_B), lambda i: (i, 0)),
            pl.BlockSpec((IN_TM, 2 * D_MODEL), lambda i: (i, 0)),
        ],
        out_shape=[
            jax.ShapeDtypeStruct((t, 3 * W_A), F32),
            jax.ShapeDtypeStruct((t, W_B), BF16),
            jax.ShapeDtypeStruct((SB_HEADS * t, W_B), BF16),
            jax.ShapeDtypeStruct((SB_HEADS * t, W_B), BF16),
            jax.ShapeDtypeStruct((t, 2 * D_MODEL), BF16),
        ],
        compiler_params=pltpu.CompilerParams(dimension_semantics=("arbitrary",), vmem_limit_bytes=VMEM_LIMIT),
        name="in_proj",
    )(x, g.reshape(1, D_MODEL), w)


A_TILE = WIN * max(d for _, d in DSW_GROUPS)


def _alibi_bias():
    slopes = 2.0 ** (-8.0 * np.arange(1, DSW_HEADS + 1) / DSW_HEADS)
    qi = np.arange(WIN)[:, None]
    kj = np.arange(2 * WIN)[None, :]
    delta = qi + WIN - kj
    valid = (delta >= 0) & (delta <= WIN)
    out = np.empty((N_GROUPS, HEADS_PER_GROUP, 2, WIN, 2 * WIN), np.float32)
    for g, (_, dil) in enumerate(DSW_GROUPS):
        for h in range(HEADS_PER_GROUP):
            b = -slopes[g * HEADS_PER_GROUP + h] * (delta * dil).astype(np.float32)
            out[g, h, 0] = np.where(valid, b, MASKED)
            out[g, h, 1] = np.where(valid & (kj >= WIN), b, MASKED)
    return out


def _mixer_a_kernel(*refs):
    q_refs = refs[0:3]
    kc_refs = refs[3:6]
    vc_refs = refs[6:9]
    kp_refs = refs[9:12]
    vp_refs = refs[12:15]
    bias_ref = refs[15]
    o_ref = refs[16]
    acc_s, m_s, l_s = refs[17:20]

    first = (pl.program_id(1) == 0).astype(jnp.int32)
    lane = lax.broadcasted_iota(jnp.int32, (WIN, LANES), 1)
    head0 = lane < HEAD_DIM

    for g, (_, dil) in enumerate(DSW_GROUPS):
        q_ref, kc_ref, vc_ref = q_refs[g], kc_refs[g], vc_refs[g]

        def rows(ref, start, dil=dil):
            if dil == 1:
                return ref[pl.ds(start, WIN), :]
            return ref[pl.ds(start, WIN, stride=dil), :]

        def unit(kprev_ref, vprev_ref, prev_off, off, sel, g=g, dil=dil, q_ref=q_ref, kc_ref=kc_ref,
                 vc_ref=vc_ref, rows=rows):
            q = rows(q_ref, off) * QK_SCALE
            k_p = rows(kprev_ref, prev_off).astype(BF16)
            k_c = rows(kc_ref, off).astype(BF16)
            v_p = rows(vprev_ref, prev_off).astype(BF16)
            v_c = rows(vc_ref, off).astype(BF16)
            per_head = []
            for h in range(2):
                qh = jnp.where(head0 if h == 0 else jnp.logical_not(head0), q, 0.0).astype(BF16)
                s_p = _dot_nt(qh, k_p) + bias_ref[g, h, sel, :, 0:WIN]
                s_c = _dot_nt(qh, k_c) + bias_ref[g, h, sel, :, WIN:2 * WIN]
                m = jnp.maximum(jnp.max(s_p, axis=-1, keepdims=True), jnp.max(s_c, axis=-1, keepdims=True))
                p_p = jnp.exp(s_p - m)
                p_c = jnp.exp(s_c - m)
                l = jnp.sum(p_p, axis=-1, keepdims=True) + jnp.sum(p_c, axis=-1, keepdims=True)
                pv = _dot(p_p.astype(BF16), v_p) + _dot(p_c.astype(BF16), v_c)
                per_head.append((m, l, pv))
            (m0, l0, pv0), (m1, l1, pv1) = per_head
            dst = pl.ds(off, WIN) if dil == 1 else pl.ds(off, WIN, stride=dil)
            acc_s[g, dst, :] = jnp.where(head0, pv0, pv1)
            m_s[g, dst, :] = jnp.where(head0, m0, m1)
            l_s[g, dst, :] = jnp.where(head0, l0, l1)

        def head_body(r, c, unit=unit, kp_ref=kp_refs[g], vp_ref=vp_refs[g]):
            unit(kp_ref, vp_ref, r, r, first)
            return c

        lax.fori_loop(0, dil, head_body, 0)

        n_units = A_TILE // WIN
        if n_units > dil:
            shift = dil.bit_length() - 1

            def tail_body(idx, c, unit=unit, kc_ref=kc_ref, vc_ref=vc_ref, dil=dil, shift=shift):
                u = lax.shift_right_logical(idx, shift)
                r = jnp.bitwise_and(idx, dil - 1)
                off = u * (WIN * dil) + r
                unit(kc_ref, vc_ref, off - WIN * dil, off, 0)
                return c

            lax.fori_loop(dil, n_units, tail_body, 0)

    m_all = jnp.maximum(jnp.maximum(m_s[0], m_s[1]), m_s[2])
    num = jnp.zeros((A_TILE, LANES), F32)
    den = jnp.zeros((A_TILE, LANES), F32)
    for g in range(N_GROUPS):
        w = jnp.exp(m_s[g] - m_all)
        num = num + w * acc_s[g]
        den = den + w * l_s[g]
    o_ref[...] = (num / den).astype(o_ref.dtype)


def _mixer_a(pa, bias, batch, seq):
    t = pa.shape[0]
    tiles = seq // A_TILE
    qcol = lambda g: (lambda b, n, hp, g=g: (b * tiles + n, 0 * 6 + 2 * g + hp))
    kcol = lambda g: (lambda b, n, hp, g=g: (b * tiles + n, 1 * 6 + 2 * g + hp))
    vcol = lambda g: (lambda b, n, hp, g=g: (b * tiles + n, 2 * 6 + 2 * g + hp))

    def prev(kind, g):
        per_tile = A_TILE // (WIN * DSW_GROUPS[g][1])
        return lambda b, n, hp: (jnp.maximum((b * tiles + n) * per_tile - 1, 0), kind * 6 + 2 * g + hp)

    in_specs = (
        [pl.BlockSpec((A_TILE, LANES), qcol(g)) for g in range(N_GROUPS)]
        + [pl.BlockSpec((A_TILE, LANES), kcol(g)) for g in range(N_GROUPS)]
        + [pl.BlockSpec((A_TILE, LANES), vcol(g)) for g in range(N_GROUPS)]
        + [pl.BlockSpec((WIN * DSW_GROUPS[g][1], LANES), prev(1, g)) for g in range(N_GROUPS)]
        + [pl.BlockSpec((WIN * DSW_GROUPS[g][1], LANES), prev(2, g)) for g in range(N_GROUPS)]
        + [pl.BlockSpec((N_GROUPS, 2, 2, WIN, 2 * WIN), lambda b, n, hp: (0, hp, 0, 0, 0))]
    )
    return pl.pallas_call(
        _mixer_a_kernel,
        grid=(batch, tiles, 2),
        in_specs=in_specs,
        out_specs=pl.BlockSpec((A_TILE, LANES), lambda b, n, hp: (b * tiles + n, hp)),
        out_shape=jax.ShapeDtypeStruct((t, OUT_A), BF16),
        scratch_shapes=[pltpu.VMEM((N_GROUPS, A_TILE, LANES), F32)] * 3,
        compiler_params=pltpu.CompilerParams(
            dimension_semantics=("arbitrary", "arbitrary", "arbitrary"), vmem_limit_bytes=VMEM_LIMIT),
        name="mixer_a",
    )(*([pa] * 15), bias)


def _suffix_sum_matrix():
    j = np.arange(B_BLK)[:, None]
    s = np.arange(B_BLK)[None, :]
    later = (j > s).astype(np.float32)
    ones = np.ones((B_BLK, B_BLK), np.float32)
    zero = np.zeros((B_BLK, B_BLK), np.float32)
    return np.block([[later, zero, ones, zero], [zero, later, zero, ones]])


def _mixer_b_kernel(q_ref, kbd_ref, vbd_ref, tri_ref, o_ref, acc_s, run_s):
    i = pl.program_id(1)
    q = q_ref[...]
    tri = tri_ref[...]
    t_loc = lax.broadcasted_iota(jnp.int32, (B_TQ, B_CAT), 0)
    s_loc = jnp.bitwise_and(lax.broadcasted_iota(jnp.int32, (B_TQ, B_CAT), 1), B_BLK - 1)

    acc_s[...] = jnp.zeros_like(acc_s)
    run_s[...] = jnp.zeros_like(run_s)

    def block(j, diag_offset):
        start = pl.multiple_of(j * B_CAT, B_CAT)
        zn = _dot_nt(q, kbd_ref[pl.ds(start, B_CAT), :])
        soft = jnp.log2(1.0 + jnp.exp2(-jnp.abs(zn)))
        log_stay = jnp.minimum(zn, 0.0) - soft
        if diag_offset is not None:
            causal = s_loc + diag_offset < t_loc
            log_stay = jnp.where(causal, log_stay, 0.0)
        ls16 = log_stay.astype(BF16)
        half = SB_HEADS // 2 * B_BLK
        sums = [_dot(ls16[:, p * half:(p + 1) * half], tri) for p in range(2)]
        later = jnp.concatenate([s[:, 0:half] for s in sums], axis=1)
        whole = jnp.concatenate([s[:, half:2 * half] for s in sums], axis=1)
        run = run_s[...]
        p = jnp.exp2((log_stay - zn) + later + run)
        if diag_offset is not None:
            p = jnp.where(causal, p, 0.0)
        run_s[...] = run + whole
        acc_s[...] += _dot(p.astype(BF16), vbd_ref[pl.ds(start, B_CAT), :])

    per_q = B_TQ // B_BLK
    for d in range(per_q - 1, -1, -1):
        block(i * per_q + d, d * B_BLK)

    def body(jj, c):
        for d in range(per_q):
            block((i - jj) * per_q - 1 - d, None)
        return c

    lax.fori_loop(0, i, body, 0)
    o_ref[...] = acc_s[...].astype(o_ref.dtype)


def _mixer_b(qb, kbd, vbd, tri, batch, seq):
    t = qb.shape[0]
    nq = seq // B_TQ
    whole_seq = pl.BlockSpec((None, SB_HEADS * seq, W_B), lambda b, i: (b, 0, 0), pipeline_mode=pl.Buffered(1))
    out = pl.pallas_call(
        _mixer_b_kernel,
        grid=(batch, nq),
        in_specs=[
            pl.BlockSpec((None, B_TQ, W_B), lambda b, i: (b, i, 0)),
            whole_seq,
            whole_seq,
            pl.BlockSpec((2 * B_BLK, 4 * B_BLK), lambda b, i: (0, 0)),
        ],
        out_specs=pl.BlockSpec((None, B_TQ, W_B), lambda b, i: (b, i, 0)),
        out_shape=jax.ShapeDtypeStruct((batch, seq, W_B), BF16),
        scratch_shapes=[pltpu.VMEM((B_TQ, W_B), F32), pltpu.VMEM((B_TQ, B_CAT), F32)],
        compiler_params=pltpu.CompilerParams(
            dimension_semantics=("arbitrary", "arbitrary"), vmem_limit_bytes=VMEM_LIMIT),
        name="mixer_b",
    )(qb.reshape(batch, seq, W_B), kbd.reshape(batch, SB_HEADS * seq, W_B),
      vbd.reshape(batch, SB_HEADS * seq, W_B), tri)
    return out.reshape(t, W_B)


MERGE_TM = 512


def _merge_kernel(x_ref, oa_ref, ob_ref, pg_ref, bg_ref, wa_ref, wb_ref, wo_ref, o_ref):
    gates = jax.nn.sigmoid(pg_ref[...].astype(F32) + bg_ref[...])
    ya = _dot(oa_ref[...], wa_ref[...])
    yb = _dot(ob_ref[...], wb_ref[...])
    merged = gates[:, 0:D_MODEL] * ya + gates[:, D_MODEL:2 * D_MODEL] * yb
    o_ref[...] = x_ref[...] + _dot(merged.astype(BF16), wo_ref[...])


def _merge(x, oa, ob, pg, bg, wa, wb, wo):
    t = x.shape[0]
    row = lambda w: pl.BlockSpec((MERGE_TM, w), lambda i: (i, 0))
    full = lambda r, c: pl.BlockSpec((r, c), lambda i: (0, 0))
    return pl.pallas_call(
        _merge_kernel,
        grid=(t // MERGE_TM,),
        in_specs=[row(D_MODEL), row(OUT_A), row(W_B), row(2 * D_MODEL), full(1, 2 * D_MODEL),
                  full(OUT_A, D_MODEL), full(W_B, D_MODEL), full(D_MODEL, D_MODEL)],
        out_specs=row(D_MODEL),
        out_shape=jax.ShapeDtypeStruct((t, D_MODEL), F32),
        compiler_params=pltpu.CompilerParams(dimension_semantics=("arbitrary",), vmem_limit_bytes=VMEM_LIMIT),
        name="merge",
    )(x, oa, ob, pg, bg.reshape(1, 2 * D_MODEL), wa, wb, wo)


FFN_TM = 1024
FFN_TF = 256
HALO = 16


def _ffn_kernel(x_ref, xh_ref, g_ref, wg_ref, wv_ref, cw_ref, cb_ref, wd_ref, gf_ref, o_ref,
                h_s, up_s, acc_s, *, tiles_per_seq, final_norm):
    i = pl.program_id(0)
    f = pl.program_id(1)

    @pl.when(f == 0)
    def _():
        g = g_ref[...]
        h_s[0:HALO, :] = _rms(xh_ref[...], g).astype(BF16)
        h_s[HALO:HALO + FFN_TM, :] = _rms(x_ref[...], g).astype(BF16)
        acc_s[...] = jnp.zeros_like(acc_s)

    up = _dot(h_s[...], wg_ref[...])
    seq_start = (i % tiles_per_seq) == 0
    up_s[0:HALO, :] = jnp.where(seq_start, 0.0, up[0:HALO])
    up_s[HALO:HALO + FFN_TM, :] = up[HALO:HALO + FFN_TM]
    a = cb_ref[...]
    for tap in range(CONV_WIDTH):
        lag = CONV_WIDTH - 1 - tap
        a = a + up_s[HALO - lag:HALO - lag + FFN_TM, :] * cw_ref[tap:tap + 1, :]
    val = _dot(h_s[HALO:HALO + FFN_TM, :], wv_ref[...])
    gelu = 0.5 * a * (1.0 + lax.erf(a * (2.0 ** -0.5)))
    act = (gelu * val).astype(BF16)
    acc_s[...] += _dot(act, wd_ref[...])

    @pl.when(f == pl.num_programs(1) - 1)
    def _():
        y = x_ref[...] + acc_s[...]
        if final_norm:
            y = _rms(y, gf_ref[...])
        o_ref[...] = y


def _ffn(x, g, w_up, cw, cb, w_down, gf, seq, final_norm):
    t = x.shape[0]
    nf = D_FF // FFN_TF
    halo_blocks = FFN_TM // HALO
    kern = functools.partial(_ffn_kernel, tiles_per_seq=seq // FFN_TM, final_norm=final_norm)
    return pl.pallas_call(
        kern,
        grid=(t // FFN_TM, nf),
        in_specs=[
            pl.BlockSpec((FFN_TM, D_MODEL), lambda i, f: (i, 0)),
            pl.BlockSpec((HALO, D_MODEL), lambda i, f: (jnp.maximum(i * halo_blocks - 1, 0), 0)),
            pl.BlockSpec((1, D_MODEL), lambda i, f: (0, 0)),
            pl.BlockSpec((D_MODEL, FFN_TF), lambda i, f: (0, f)),
            pl.BlockSpec((D_MODEL, FFN_TF), lambda i, f: (0, nf + f)),
            pl.BlockSpec((CONV_WIDTH, FFN_TF), lambda i, f: (0, f)),
            pl.BlockSpec((1, FFN_TF), lambda i, f: (0, f)),
            pl.BlockSpec((FFN_TF, D_MODEL), lambda i, f: (f, 0)),
            pl.BlockSpec((1, D_MODEL), lambda i, f: (0, 0)),
        ],
        out_specs=pl.BlockSpec((FFN_TM, D_MODEL), lambda i, f: (i, 0)),
        out_shape=jax.ShapeDtypeStruct((t, D_MODEL), F32),
        scratch_shapes=[
            pltpu.VMEM((HALO + FFN_TM, D_MODEL), BF16),
            pltpu.VMEM((HALO + FFN_TM, FFN_TF), F32),
            pltpu.VMEM((FFN_TM, D_MODEL), F32),
        ],
        compiler_params=pltpu.CompilerParams(
            dimension_semantics=("arbitrary", "arbitrary"), vmem_limit_bytes=VMEM_LIMIT),
        name="ffn",
    )(x, x, g.reshape(1, D_MODEL), w_up, w_up, cw, cb.reshape(1, D_FF), w_down, gf.reshape(1, D_MODEL))


def kernel(x, norm1, w_in, b_gate, w_br, w_o, norm2, w_up, conv_w, conv_b, w_down, norm_f):
    batch, seq, d = x.shape
    depth = norm1.shape[0]
    assert d == D_MODEL and seq % A_TILE == 0 and seq % FFN_TM == 0
    t = batch * seq
    bias = jnp.asarray(_alibi_bias())
    tri = jnp.asarray(_suffix_sum_matrix(), dtype=BF16)
    xf = x.reshape(t, d)
    for l in range(depth):
        pa, qb, kbd, vbd, pg = _in_proj(xf, norm1[l], w_in[l].astype(BF16))
        oa = _mixer_a(pa, bias, batch, seq)
        ob = _mixer_b(qb, kbd, vbd, tri, batch, seq)
        wbr = w_br[l].astype(BF16)
        x1 = _merge(xf, oa, ob, pg, b_gate[l], wbr[:OUT_A], wbr[OUT_A:], w_o[l].astype(BF16))
        xf = _ffn(x1, norm2[l], w_up[l].astype(BF16), conv_w[l], conv_b[l], w_down[l].astype(BF16),
                  norm_f, seq, final_norm=(l == depth - 1))
    return xf.reshape(batch, seq, d)
```

```python
import functools

import numpy as np
import jax
import jax.numpy as jnp
from jax import lax
from jax.experimental import pallas as pl
from jax.experimental.pallas import tpu as pltpu

D_MODEL = 1024
HEAD_DIM = 64
DSW_GROUPS = ((128, 1), (512, 4), (2048, 16))
N_GROUPS = len(DSW_GROUPS)
HEADS_PER_GROUP = 4
DSW_HEADS = HEADS_PER_GROUP * N_GROUPS
SB_HEADS = 4
W_A = DSW_HEADS * HEAD_DIM
W_B = SB_HEADS * HEAD_DIM
OUT_A = HEADS_PER_GROUP * HEAD_DIM
N_IN = 3 * W_A + 3 * W_B + 2 * D_MODEL
D_FF = 2816
CONV_WIDTH = 3
RMS_EPS = 1e-6
QK_SCALE = HEAD_DIM ** -0.5

LANES = 128
WIN = 128
MASKED = -1e30

VMEM_LIMIT = 56 * 1024 * 1024

F32 = jnp.float32
BF16 = jnp.bfloat16


def _dot(a, b):
    return jnp.dot(a, b, preferred_element_type=F32)


def _dot_nt(a, b):
    return lax.dot_general(a, b, (((1,), (1,)), ((), ())), preferred_element_type=F32)


def _rms(x, g):
    ms = jnp.mean(x * x, axis=-1, keepdims=True)
    return x * lax.rsqrt(ms + RMS_EPS) * g


IN_TM = 512
IN_CHUNK = 256
B_BLK = 128
B_TQ = 256
B_CAT = SB_HEADS * B_BLK
LOG2E = 1.4426950408889634
RUN_FLOOR = -160.0


def _in_proj_kernel(x_ref, g_ref, w_ref, pa_ref, qb_ref, kbd_ref, vbd_ref, pg_ref):
    h = _rms(x_ref[...], g_ref[...]).astype(BF16)

    def proj(base):
        return _dot(h, w_ref[:, base:base + IN_CHUNK])

    for c in range(0, 3 * W_A, IN_CHUNK):
        pa_ref[:, c:c + IN_CHUNK] = proj(c)
    base_b = 3 * W_A
    qb_ref[...] = (proj(base_b) * (-QK_SCALE * LOG2E)).astype(BF16)
    lane = lax.broadcasted_iota(jnp.int32, (B_BLK, W_B), 1)
    for ref, base in ((kbd_ref, base_b + W_B), (vbd_ref, base_b + 2 * W_B)):
        val = proj(base)
        for r in range(IN_TM // B_BLK):
            rows = val[r * B_BLK:(r + 1) * B_BLK]
            for hd in range(SB_HEADS):
                keep = (lane >= hd * HEAD_DIM) & (lane < (hd + 1) * HEAD_DIM)
                ref[(r * SB_HEADS + hd) * B_BLK:(r * SB_HEADS + hd + 1) * B_BLK, :] = (
                    jnp.where(keep, rows, 0.0).astype(BF16))
    base_g = 3 * W_A + 3 * W_B
    for c in range(0, 2 * D_MODEL, IN_CHUNK):
        pg_ref[:, c:c + IN_CHUNK] = proj(base_g + c).astype(BF16)


def _in_proj(x, g, w):
    t = x.shape[0]
    return pl.pallas_call(
        _in_proj_kernel,
        grid=(t // IN_TM,),
        in_specs=[
            pl.BlockSpec((IN_TM, D_MODEL), lambda i: (i, 0)),
            pl.BlockSpec((1, D_MODEL), lambda i: (0, 0)),
            pl.BlockSpec((D_MODEL, N_IN), lambda i: (0, 0)),
        ],
        out_specs=[
            pl.BlockSpec((IN_TM, 3 * W_A), lambda i: (i, 0)),
            pl.BlockSpec((IN_TM, W_B), lambda i: (i, 0)),
            pl.BlockSpec((SB_HEADS * IN_TM, W_B), lambda i: (i, 0)),
            pl.BlockSpec((SB_HEADS * IN_TM, W_B), lambda i: (i, 0)),
            pl.BlockSpec((IN_TM, 2 * D_MODEL), lambda i: (i, 0)),
        ],
        out_shape=[
            jax.ShapeDtypeStruct((t, 3 * W_A), F32),
            jax.ShapeDtypeStruct((t, W_B), BF16),
            jax.ShapeDtypeStruct((SB_HEADS * t, W_B), BF16),
            jax.ShapeDtypeStruct((SB_HEADS * t, W_B), BF16),
            jax.ShapeDtypeStruct((t, 2 * D_MODEL), BF16),
        ],
        compiler_params=pltpu.CompilerParams(dimension_semantics=("arbitrary",), vmem_limit_bytes=VMEM_LIMIT),
        name="in_proj",
    )(x, g.reshape(1, D_MODEL), w)


A_TILE = WIN * max(d for _, d in DSW_GROUPS)


def _alibi_bias():
    slopes = 2.0 ** (-8.0 * np.arange(1, DSW_HEADS + 1) / DSW_HEADS)
    qi = np.arange(WIN)[:, None]
    kj = np.arange(2 * WIN)[None, :]
    delta = qi + WIN - kj
    valid = (delta >= 0) & (delta <= WIN)
    out = np.empty((N_GROUPS, HEADS_PER_GROUP, 2, WIN, 2 * WIN), np.float32)
    for g, (_, dil) in enumerate(DSW_GROUPS):
        for h in range(HEADS_PER_GROUP):
            b = -slopes[g * HEADS_PER_GROUP + h] * (delta * dil).astype(np.float32)
            out[g, h, 0] = np.where(valid, b, MASKED)
            out[g, h, 1] = np.where(valid & (kj >= WIN), b, MASKED)
    return out


def _mixer_a_kernel(*refs):
    q_refs = refs[0:3]
    kc_refs = refs[3:6]
    vc_refs = refs[6:9]
    kp_refs = refs[9:12]
    vp_refs = refs[12:15]
    bias_ref = refs[15]
    o_ref = refs[16]
    acc_s, m_s, l_s = refs[17:20]

    first = (pl.program_id(1) == 0).astype(jnp.int32)
    lane = lax.broadcasted_iota(jnp.int32, (WIN, LANES), 1)
    head0 = lane < HEAD_DIM

    for g, (_, dil) in enumerate(DSW_GROUPS):
        q_ref, kc_ref, vc_ref = q_refs[g], kc_refs[g], vc_refs[g]

        def rows(ref, start, dil=dil):
            if dil == 1:
                return ref[pl.ds(start, WIN), :]
            return ref[pl.ds(start, WIN, stride=dil), :]

        def unit(kprev_ref, vprev_ref, prev_off, off, sel, g=g, dil=dil, q_ref=q_ref, kc_ref=kc_ref,
                 vc_ref=vc_ref, rows=rows):
            q = rows(q_ref, off) * QK_SCALE
            k_p = rows(kprev_ref, prev_off).astype(BF16)
            k_c = rows(kc_ref, off).astype(BF16)
            v_p = rows(vprev_ref, prev_off).astype(BF16)
            v_c = rows(vc_ref, off).astype(BF16)
            per_head = []
            for h in range(2):
                qh = jnp.where(head0 if h == 0 else jnp.logical_not(head0), q, 0.0).astype(BF16)
                s_p = _dot_nt(qh, k_p) + bias_ref[g, h, sel, :, 0:WIN]
                s_c = _dot_nt(qh, k_c) + bias_ref[g, h, sel, :, WIN:2 * WIN]
                m = jnp.maximum(jnp.max(s_p, axis=-1, keepdims=True), jnp.max(s_c, axis=-1, keepdims=True))
                p_p = jnp.exp(s_p - m)
                p_c = jnp.exp(s_c - m)
                l = jnp.sum(p_p, axis=-1, keepdims=True) + jnp.sum(p_c, axis=-1, keepdims=True)
                pv = _dot(p_p.astype(BF16), v_p) + _dot(p_c.astype(BF16), v_c)
                per_head.append((m, l, pv))
            (m0, l0, pv0), (m1, l1, pv1) = per_head
            dst = pl.ds(off, WIN) if dil == 1 else pl.ds(off, WIN, stride=dil)
            acc_s[g, dst, :] = jnp.where(head0, pv0, pv1)
            m_s[g, dst, :] = jnp.where(head0, m0, m1)
            l_s[g, dst, :] = jnp.where(head0, l0, l1)

        def head_body(r, c, unit=unit, kp_ref=kp_refs[g], vp_ref=vp_refs[g]):
            unit(kp_ref, vp_ref, r, r, first)
            return c

        lax.fori_loop(0, dil, head_body, 0)

        n_units = A_TILE // WIN
        if n_units > dil:
            shift = dil.bit_length() - 1

            def tail_body(idx, c, unit=unit, kc_ref=kc_ref, vc_ref=vc_ref, dil=dil, shift=shift):
                u = lax.shift_right_logical(idx, shift)
                r = jnp.bitwise_and(idx, dil - 1)
                off = u * (WIN * dil) + r
                unit(kc_ref, vc_ref, off - WIN * dil, off, 0)
                return c

            lax.fori_loop(dil, n_units, tail_body, 0)

    m_all = jnp.maximum(jnp.maximum(m_s[0], m_s[1]), m_s[2])
    num = jnp.zeros((A_TILE, LANES), F32)
    den = jnp.zeros((A_TILE, LANES), F32)
    for g in range(N_GROUPS):
        w = jnp.exp(m_s[g] - m_all)
        num = num + w * acc_s[g]
        den = den + w * l_s[g]
    o_ref[...] = (num / den).astype(o_ref.dtype)


def _mixer_a(pa, bias, batch, seq):
    t = pa.shape[0]
    tiles = seq // A_TILE
    qcol = lambda g: (lambda b, n, hp, g=g: (b * tiles + n, 0 * 6 + 2 * g + hp))
    kcol = lambda g: (lambda b, n, hp, g=g: (b * tiles + n, 1 * 6 + 2 * g + hp))
    vcol = lambda g: (lambda b, n, hp, g=g: (b * tiles + n, 2 * 6 + 2 * g + hp))

    def prev(kind, g):
        per_tile = A_TILE // (WIN * DSW_GROUPS[g][1])
        return lambda b, n, hp: (jnp.maximum((b * tiles + n) * per_tile - 1, 0), kind * 6 + 2 * g + hp)

    in_specs = (
        [pl.BlockSpec((A_TILE, LANES), qcol(g)) for g in range(N_GROUPS)]
        + [pl.BlockSpec((A_TILE, LANES), kcol(g)) for g in range(N_GROUPS)]
        + [pl.BlockSpec((A_TILE, LANES), vcol(g)) for g in range(N_GROUPS)]
        + [pl.BlockSpec((WIN * DSW_GROUPS[g][1], LANES), prev(1, g)) for g in range(N_GROUPS)]
        + [pl.BlockSpec((WIN * DSW_GROUPS[g][1], LANES), prev(2, g)) for g in range(N_GROUPS)]
        + [pl.BlockSpec((N_GROUPS, 2, 2, WIN, 2 * WIN), lambda b, n, hp: (0, hp, 0, 0, 0))]
    )
    return pl.pallas_call(
        _mixer_a_kernel,
        grid=(batch, tiles, 2),
        in_specs=in_specs,
        out_specs=pl.BlockSpec((A_TILE, LANES), lambda b, n, hp: (b * tiles + n, hp)),
        out_shape=jax.ShapeDtypeStruct((t, OUT_A), BF16),
        scratch_shapes=[pltpu.VMEM((N_GROUPS, A_TILE, LANES), F32)] * 3,
        compiler_params=pltpu.CompilerParams(
            dimension_semantics=("arbitrary", "arbitrary", "arbitrary"), vmem_limit_bytes=VMEM_LIMIT),
        name="mixer_a",
    )(*([pa] * 15), bias)


def _suffix_sum_matrix():
    j = np.arange(B_BLK)[:, None]
    s = np.arange(B_BLK)[None, :]
    later = (j > s).astype(np.float32)
    ones = np.ones((B_BLK, B_BLK), np.float32)
    zero = np.zeros((B_BLK, B_BLK), np.float32)
    return np.block([[later, zero, ones, zero], [zero, later, zero, ones]])


def _mixer_b_kernel(q_ref, kbd_ref, vbd_ref, tri_ref, o_ref, acc_s, run_s):
    i = pl.program_id(1)
    q = q_ref[...]
    tri = tri_ref[...]
    t_loc = lax.broadcasted_iota(jnp.int32, (B_TQ, B_CAT), 0)
    s_loc = jnp.bitwise_and(lax.broadcasted_iota(jnp.int32, (B_TQ, B_CAT), 1), B_BLK - 1)

    acc_s[...] = jnp.zeros_like(acc_s)
    run_s[...] = jnp.zeros_like(run_s)

    def block(j, diag_offset):
        start = pl.multiple_of(j * B_CAT, B_CAT)
        zn = _dot_nt(q, kbd_ref[pl.ds(start, B_CAT), :])
        soft = jnp.log2(1.0 + jnp.exp2(-jnp.abs(zn)))
        log_stay = jnp.minimum(zn, 0.0) - soft
        if diag_offset is not None:
            causal = s_loc + diag_offset < t_loc
            log_stay = jnp.where(causal, log_stay, 0.0)
        ls16 = log_stay.astype(BF16)
        half = SB_HEADS // 2 * B_BLK
        sums = [_dot(ls16[:, p * half:(p + 1) * half], tri) for p in range(2)]
        later = jnp.concatenate([s[:, 0:half] for s in sums], axis=1)
        whole = jnp.concatenate([s[:, half:2 * half] for s in sums], axis=1)
        run = run_s[...]
        p = jnp.exp2((log_stay - zn) + later + run)
        if diag_offset is not None:
            p = jnp.where(causal, p, 0.0)
        run_s[...] = run + whole
        acc_s[...] += _dot(p.astype(BF16), vbd_ref[pl.ds(start, B_CAT), :])

    per_q = B_TQ // B_BLK
    for d in range(per_q - 1, -1, -1):
        block(i * per_q + d, d * B_BLK)

    def cond(c):
        jj, live = c
        return jnp.logical_and(jj < i, live)

    def body(c):
        jj, _ = c
        for d in range(per_q):
            block((i - jj) * per_q - 1 - d, None)
        run = run_s[...]
        top = run[:, 0:B_BLK]
        for h in range(1, SB_HEADS):
            top = jnp.maximum(top, run[:, h * B_BLK:(h + 1) * B_BLK])
        return jj + 1, jnp.max(top) > RUN_FLOOR

    lax.while_loop(cond, body, (jnp.int32(0), jnp.bool_(True)))
    o_ref[...] = acc_s[...].astype(o_ref.dtype)


def _mixer_b(qb, kbd, vbd, tri, batch, seq):
    t = qb.shape[0]
    nq = seq // B_TQ
    whole_seq = pl.BlockSpec((None, SB_HEADS * seq, W_B), lambda b, i: (b, 0, 0), pipeline_mode=pl.Buffered(1))
    out = pl.pallas_call(
        _mixer_b_kernel,
        grid=(batch, nq),
        in_specs=[
            pl.BlockSpec((None, B_TQ, W_B), lambda b, i: (b, i, 0)),
            whole_seq,
            whole_seq,
            pl.BlockSpec((2 * B_BLK, 4 * B_BLK), lambda b, i: (0, 0)),
        ],
        out_specs=pl.BlockSpec((None, B_TQ, W_B), lambda b, i: (b, i, 0)),
        out_shape=jax.ShapeDtypeStruct((batch, seq, W_B), BF16),
        scratch_shapes=[pltpu.VMEM((B_TQ, W_B), F32), pltpu.VMEM((B_TQ, B_CAT), F32)],
        compiler_params=pltpu.CompilerParams(
            dimension_semantics=("arbitrary", "arbitrary"), vmem_limit_bytes=VMEM_LIMIT),
        name="mixer_b",
    )(qb.reshape(batch, seq, W_B), kbd.reshape(batch, SB_HEADS * seq, W_B),
      vbd.reshape(batch, SB_HEADS * seq, W_B), tri)
    return out.reshape(t, W_B)


MERGE_TM = 512


def _merge_kernel(x_ref, oa_ref, ob_ref, pg_ref, bg_ref, wa_ref, wb_ref, wo_ref, o_ref):
    gates = jax.nn.sigmoid(pg_ref[...].astype(F32) + bg_ref[...])
    ya = _dot(oa_ref[...], wa_ref[...])
    yb = _dot(ob_ref[...], wb_ref[...])
    merged = gates[:, 0:D_MODEL] * ya + gates[:, D_MODEL:2 * D_MODEL] * yb
    o_ref[...] = x_ref[...] + _dot(merged.astype(BF16), wo_ref[...])


def _merge(x, oa, ob, pg, bg, wa, wb, wo):
    t = x.shape[0]
    row = lambda w: pl.BlockSpec((MERGE_TM, w), lambda i: (i, 0))
    full = lambda r, c: pl.BlockSpec((r, c), lambda i: (0, 0))
    return pl.pallas_call(
        _merge_kernel,
        grid=(t // MERGE_TM,),
        in_specs=[row(D_MODEL), row(OUT_A), row(W_B), row(2 * D_MODEL), full(1, 2 * D_MODEL),
                  full(OUT_A, D_MODEL), full(W_B, D_MODEL), full(D_MODEL, D_MODEL)],
        out_specs=row(D_MODEL),
        out_shape=jax.ShapeDtypeStruct((t, D_MODEL), F32),
        compiler_params=pltpu.CompilerParams(dimension_semantics=("arbitrary",), vmem_limit_bytes=VMEM_LIMIT),
        name="merge",
    )(x, oa, ob, pg, bg.reshape(1, 2 * D_MODEL), wa, wb, wo)


FFN_TM = 1024
FFN_TF = 256
HALO = 16


def _ffn_kernel(x_ref, xh_ref, g_ref, wg_ref, wv_ref, cw_ref, cb_ref, wd_ref, gf_ref, o_ref,
                h_s, up_s, acc_s, *, tiles_per_seq, final_norm):
    i = pl.program_id(0)
    f = pl.program_id(1)

    @pl.when(f == 0)
    def _():
        g = g_ref[...]
        h_s[0:HALO, :] = _rms(xh_ref[...], g).astype(BF16)
        h_s[HALO:HALO + FFN_TM, :] = _rms(x_ref[...], g).astype(BF16)
        acc_s[...] = jnp.zeros_like(acc_s)

    up = _dot(h_s[...], wg_ref[...])
    seq_start = (i % tiles_per_seq) == 0
    up_s[0:HALO, :] = jnp.where(seq_start, 0.0, up[0:HALO])
    up_s[HALO:HALO + FFN_TM, :] = up[HALO:HALO + FFN_TM]
    a = cb_ref[...]
    for tap in range(CONV_WIDTH):
        lag = CONV_WIDTH - 1 - tap
        a = a + up_s[HALO - lag:HALO - lag + FFN_TM, :] * cw_ref[tap:tap + 1, :]
    val = _dot(h_s[HALO:HALO + FFN_TM, :], wv_ref[...])
    gelu = 0.5 * a * (1.0 + lax.erf(a * (2.0 ** -0.5)))
    act = (gelu * val).astype(BF16)
    acc_s[...] += _dot(act, wd_ref[...])

    @pl.when(f == pl.num_programs(1) - 1)
    def _():
        y = x_ref[...] + acc_s[...]
        if final_norm:
            y = _rms(y, gf_ref[...])
        o_ref[...] = y


def _ffn(x, g, w_up, cw, cb, w_down, gf, seq, final_norm):
    t = x.shape[0]
    nf = D_FF // FFN_TF
    halo_blocks = FFN_TM // HALO
    kern = functools.partial(_ffn_kernel, tiles_per_seq=seq // FFN_TM, final_norm=final_norm)
    return pl.pallas_call(
        kern,
        grid=(t // FFN_TM, nf),
        in_specs=[
            pl.BlockSpec((FFN_TM, D_MODEL), lambda i, f: (i, 0)),
            pl.BlockSpec((HALO, D_MODEL), lambda i, f: (jnp.maximum(i * halo_blocks - 1, 0), 0)),
            pl.BlockSpec((1, D_MODEL), lambda i, f: (0, 0)),
            pl.BlockSpec((D_MODEL, FFN_TF), lambda i, f: (0, f)),
            pl.BlockSpec((D_MODEL, FFN_TF), lambda i, f: (0, nf + f)),
            pl.BlockSpec((CONV_WIDTH, FFN_TF), lambda i, f: (0, f)),
            pl.BlockSpec((1, FFN_TF), lambda i, f: (0, f)),
            pl.BlockSpec((FFN_TF, D_MODEL), lambda i, f: (f, 0)),
            pl.BlockSpec((1, D_MODEL), lambda i, f: (0, 0)),
        ],
        out_specs=pl.BlockSpec((FFN_TM, D_MODEL), lambda i, f: (i, 0)),
        out_shape=jax.ShapeDtypeStruct((t, D_MODEL), F32),
        scratch_shapes=[
            pltpu.VMEM((HALO + FFN_TM, D_MODEL), BF16),
            pltpu.VMEM((HALO + FFN_TM, FFN_TF), F32),
            pltpu.VMEM((FFN_TM, D_MODEL), F32),
        ],
        compiler_params=pltpu.CompilerParams(
            dimension_semantics=("arbitrary", "arbitrary"), vmem_limit_bytes=VMEM_LIMIT),
        name="ffn",
    )(x, x, g.reshape(1, D_MODEL), w_up, w_up, cw, cb.reshape(1, D_FF), w_down, gf.reshape(1, D_MODEL))


def kernel(x, norm1, w_in, b_gate, w_br, w_o, norm2, w_up, conv_w, conv_b, w_down, norm_f):
    batch, seq, d = x.shape
    depth = norm1.shape[0]
    assert d == D_MODEL and seq % A_TILE == 0 and seq % FFN_TM == 0
    t = batch * seq
    bias = jnp.asarray(_alibi_bias())
    tri = jnp.asarray(_suffix_sum_matrix(), dtype=BF16)
    xf = x.reshape(t, d)
    for l in range(depth):
        pa, qb, kbd, vbd, pg = _in_proj(xf, norm1[l], w_in[l].astype(BF16))
        oa = _mixer_a(pa, bias, batch, seq)
        ob = _mixer_b(qb, kbd, vbd, tri, batch, seq)
        wbr = w_br[l].astype(BF16)
        x1 = _merge(xf, oa, ob, pg, b_gate[l], wbr[:OUT_A], wbr[OUT_A:], w_o[l].astype(BF16))
        xf = _ffn(x1, norm2[l], w_up[l].astype(BF16), conv_w[l], conv_b[l], w_down[l].astype(BF16),
                  norm_f, seq, final_norm=(l == depth - 1))
    return xf.reshape(batch, seq, d)
```

```python
import functools

import numpy as np
import jax
import jax.numpy as jnp
from jax import lax
from jax.experimental import pallas as pl
from jax.experimental.pallas import tpu as pltpu

D_MODEL = 1024
HEAD_DIM = 64
DSW_GROUPS = ((128, 1), (512, 4), (2048, 16))
N_GROUPS = len(DSW_GROUPS)
HEADS_PER_GROUP = 4
DSW_HEADS = HEADS_PER_GROUP * N_GROUPS
SB_HEADS = 4
W_A = DSW_HEADS * HEAD_DIM
W_B = SB_HEADS * HEAD_DIM
OUT_A = HEADS_PER_GROUP * HEAD_DIM
N_IN = 3 * W_A + 3 * W_B + 2 * D_MODEL
D_FF = 2816
CONV_WIDTH = 3
RMS_EPS = 1e-6
QK_SCALE = HEAD_DIM ** -0.5

LANES = 128
WIN = 128
MASKED = -1e30

VMEM_LIMIT = 56 * 1024 * 1024

F32 = jnp.float32
BF16 = jnp.bfloat16


def _dot(a, b):
    return jnp.dot(a, b, preferred_element_type=F32)


def _dot_nt(a, b):
    return lax.dot_general(a, b, (((1,), (1,)), ((), ())), preferred_element_type=F32)


def _rms(x, g):
    ms = jnp.mean(x * x, axis=-1, keepdims=True)
    return x * lax.rsqrt(ms + RMS_EPS) * g


IN_TM = 512
IN_CHUNK = 256
B_BLK = 128
B_TQ = 256
B_CAT = SB_HEADS * B_BLK
LOG2E = 1.4426950408889634
RUN_FLOOR = -160.0


def _in_proj_kernel(x_ref, g_ref, w_ref, pa_ref, qb_ref, kbd_ref, vbd_ref, pg_ref):
    h = _rms(x_ref[...], g_ref[...]).astype(BF16)

    def proj(base):
        return _dot(h, w_ref[:, base:base + IN_CHUNK])

    for c in range(0, 3 * W_A, IN_CHUNK):
        pa_ref[:, c:c + IN_CHUNK] = proj(c)
    base_b = 3 * W_A
    qb_ref[...] = (proj(base_b) * (-QK_SCALE * LOG2E)).astype(BF16)
    lane = lax.broadcasted_iota(jnp.int32, (B_BLK, W_B), 1)
    for ref, base in ((kbd_ref, base_b + W_B), (vbd_ref, base_b + 2 * W_B)):
        val = proj(base)
        for r in range(IN_TM // B_BLK):
            rows = val[r * B_BLK:(r + 1) * B_BLK]
            for hd in range(SB_HEADS):
                keep = (lane >= hd * HEAD_DIM) & (lane < (hd + 1) * HEAD_DIM)
                ref[(r * SB_HEADS + hd) * B_BLK:(r * SB_HEADS + hd + 1) * B_BLK, :] = (
                    jnp.where(keep, rows, 0.0).astype(BF16))
    base_g = 3 * W_A + 3 * W_B
    for c in range(0, 2 * D_MODEL, IN_CHUNK):
        pg_ref[:, c:c + IN_CHUNK] = proj(base_g + c).astype(BF16)


def _in_proj(x, g, w):
    t = x.shape[0]
    return pl.pallas_call(
        _in_proj_kernel,
        grid=(t // IN_TM,),
        in_specs=[
            pl.BlockSpec((IN_TM, D_MODEL), lambda i: (i, 0)),
            pl.BlockSpec((1, D_MODEL), lambda i: (0, 0)),
            pl.BlockSpec((D_MODEL, N_IN), lambda i: (0, 0)),
        ],
        out_specs=[
            pl.BlockSpec((IN_TM, 3 * W_A), lambda i: (i, 0)),
            pl.BlockSpec((IN_TM, W_B), lambda i: (i, 0)),
            pl.BlockSpec((SB_HEADS * IN_TM, W_B), lambda i: (i, 0)),
            pl.BlockSpec((SB_HEADS * IN_TM, W_B), lambda i: (i, 0)),
            pl.BlockSpec((IN_TM, 2 * D_MODEL), lambda i: (i, 0)),
        ],
        out_shape=[
            jax.ShapeDtypeStruct((t, 3 * W_A), F32),
            jax.ShapeDtypeStruct((t, W_B), BF16),
            jax.ShapeDtypeStruct((SB_HEADS * t, W_B), BF16),
            jax.ShapeDtypeStruct((SB_HEADS * t, W_B), BF16),
            jax.ShapeDtypeStruct((t, 2 * D_MODEL), BF16),
        ],
        compiler_params=pltpu.CompilerParams(dimension_semantics=("arbitrary",), vmem_limit_bytes=VMEM_LIMIT),
        name="in_proj",
    )(x, g.reshape(1, D_MODEL), w)


A_TILE = WIN * max(d for _, d in DSW_GROUPS)


def _alibi_bias():
    slopes = 2.0 ** (-8.0 * np.arange(1, DSW_HEADS + 1) / DSW_HEADS)
    qi = np.arange(WIN)[:, None]
    kj = np.arange(2 * WIN)[None, :]
    delta = qi + WIN - kj
    valid = (delta >= 0) & (delta <= WIN)
    out = np.empty((N_GROUPS, 2, 2, 2 * WIN, 2 * WIN), np.float32)
    for g, (_, dil) in enumerate(DSW_GROUPS):
        for h in range(HEADS_PER_GROUP):
            b = -slopes[g * HEADS_PER_GROUP + h] * LOG2E * (delta * dil).astype(np.float32)
            rows = slice((h % 2) * WIN, (h % 2 + 1) * WIN)
            out[g, h // 2, 0, rows] = np.where(valid, b, MASKED)
            out[g, h // 2, 1, rows] = np.where(valid & (kj >= WIN), b, MASKED)
    return out


A_UNROLL = {1: 3, 4: 4, 16: 4}


def _mixer_a_kernel(*refs):
    q_refs = refs[0:3]
    kc_refs = refs[3:6]
    vc_refs = refs[6:9]
    kp_refs = refs[9:12]
    vp_refs = refs[12:15]
    bias_ref = refs[15]
    o_ref = refs[16]
    acc_s, m_s, l_s = refs[17:20]

    first = (pl.program_id(1) == 0).astype(jnp.int32)
    lane = lax.broadcasted_iota(jnp.int32, (WIN, LANES), 1)
    head0 = lane < HEAD_DIM
    ones = jnp.ones((2 * WIN, LANES), BF16)
    n_units = A_TILE // WIN

    for g, (_, dil) in enumerate(DSW_GROUPS):
        q_ref, kc_ref, vc_ref = q_refs[g], kc_refs[g], vc_refs[g]

        def rows(ref, start, dil=dil):
            if dil == 1:
                return ref[pl.ds(start, WIN), :]
            return ref[pl.ds(start, WIN, stride=dil), :]

        def unit(kprev_ref, vprev_ref, prev_off, off, sel, g=g, dil=dil, q_ref=q_ref, kc_ref=kc_ref,
                 vc_ref=vc_ref, rows=rows):
            q = rows(q_ref, off) * (QK_SCALE * LOG2E)
            q2 = jnp.concatenate([jnp.where(head0, q, 0.0), jnp.where(head0, 0.0, q)], axis=0).astype(BF16)
            k = jnp.concatenate([rows(kprev_ref, prev_off), rows(kc_ref, off)], axis=0).astype(BF16)
            v = jnp.concatenate([rows(vprev_ref, prev_off), rows(vc_ref, off)], axis=0).astype(BF16)
            s = _dot_nt(q2, k) + bias_ref[g, sel]
            m = jnp.max(s, axis=-1, keepdims=True)
            p = jnp.exp2(s - m).astype(BF16)
            pvl = _dot(p, jnp.concatenate([v, ones], axis=1))
            dst = pl.ds(off, WIN) if dil == 1 else pl.ds(off, WIN, stride=dil)
            acc_s[g, dst, :] = jnp.where(head0, pvl[0:WIN, 0:LANES], pvl[WIN:2 * WIN, 0:LANES])
            l_s[g, dst, :] = jnp.where(head0, pvl[0:WIN, LANES:2 * LANES], pvl[WIN:2 * WIN, LANES:2 * LANES])
            m_s[g, dst, :] = jnp.where(head0, m[0:WIN], m[WIN:2 * WIN])

        def head_unit(r, unit=unit, kp_ref=kp_refs[g], vp_ref=vp_refs[g]):
            unit(kp_ref, vp_ref, r, r, first)

        def tail_unit(idx, unit=unit, kc_ref=kc_ref, vc_ref=vc_ref, dil=dil):
            shift = dil.bit_length() - 1
            u = idx >> shift if isinstance(idx, int) else lax.shift_right_logical(idx, shift)
            off = u * (WIN * dil) + (idx & (dil - 1))
            unit(kc_ref, vc_ref, off - WIN * dil, off, 0)

        unroll = A_UNROLL[dil]
        for fn, lo, hi in ((head_unit, 0, dil), (tail_unit, dil, n_units)):
            count = hi - lo
            if count == 0:
                continue
            if count <= unroll:
                for idx in range(lo, hi):
                    fn(idx)
                continue
            assert count % unroll == 0

            def body(t, c, fn=fn, lo=lo, unroll=unroll):
                for uu in range(unroll):
                    fn(lo + t * unroll + uu)
                return c

            lax.fori_loop(0, count // unroll, body, 0)

    m_all = jnp.maximum(jnp.maximum(m_s[0], m_s[1]), m_s[2])
    num = jnp.zeros((A_TILE, LANES), F32)
    den = jnp.zeros((A_TILE, LANES), F32)
    for g in range(N_GROUPS):
        w = jnp.exp2(m_s[g] - m_all)
        num = num + w * acc_s[g]
        den = den + w * l_s[g]
    o_ref[...] = (num / den).astype(o_ref.dtype)


def _mixer_a(pa, bias, batch, seq):
    t = pa.shape[0]
    tiles = seq // A_TILE
    qcol = lambda g: (lambda b, n, hp, g=g: (b * tiles + n, 0 * 6 + 2 * g + hp))
    kcol = lambda g: (lambda b, n, hp, g=g: (b * tiles + n, 1 * 6 + 2 * g + hp))
    vcol = lambda g: (lambda b, n, hp, g=g: (b * tiles + n, 2 * 6 + 2 * g + hp))

    def prev(kind, g):
        per_tile = A_TILE // (WIN * DSW_GROUPS[g][1])
        return lambda b, n, hp: (jnp.maximum((b * tiles + n) * per_tile - 1, 0), kind * 6 + 2 * g + hp)

    in_specs = (
        [pl.BlockSpec((A_TILE, LANES), qcol(g)) for g in range(N_GROUPS)]
        + [pl.BlockSpec((A_TILE, LANES), kcol(g)) for g in range(N_GROUPS)]
        + [pl.BlockSpec((A_TILE, LANES), vcol(g)) for g in range(N_GROUPS)]
        + [pl.BlockSpec((WIN * DSW_GROUPS[g][1], LANES), prev(1, g)) for g in range(N_GROUPS)]
        + [pl.BlockSpec((WIN * DSW_GROUPS[g][1], LANES), prev(2, g)) for g in range(N_GROUPS)]
        + [pl.BlockSpec((N_GROUPS, None, 2, 2 * WIN, 2 * WIN), lambda b, n, hp: (0, hp, 0, 0, 0))]
    )
    return pl.pallas_call(
        _mixer_a_kernel,
        grid=(batch, tiles, 2),
        in_specs=in_specs,
        out_specs=pl.BlockSpec((A_TILE, LANES), lambda b, n, hp: (b * tiles + n, hp)),
        out_shape=jax.ShapeDtypeStruct((t, OUT_A), BF16),
        scratch_shapes=[pltpu.VMEM((N_GROUPS, A_TILE, LANES), F32)] * 3,
        compiler_params=pltpu.CompilerParams(
            dimension_semantics=("arbitrary", "arbitrary", "arbitrary"), vmem_limit_bytes=VMEM_LIMIT),
        name="mixer_a",
    )(*([pa] * 15), bias)


def _suffix_sum_matrix():
    j = np.arange(B_BLK)[:, None]
    s = np.arange(B_BLK)[None, :]
    later = (j > s).astype(np.float32)
    ones = np.ones((B_BLK, B_BLK), np.float32)
    zero = np.zeros((B_BLK, B_BLK), np.float32)
    return np.block([[later, zero, ones, zero], [zero, later, zero, ones]])


def _mixer_b_kernel(q_ref, kbd_ref, vbd_ref, tri_ref, o_ref, acc_s, run_s):
    i = pl.program_id(1)
    q = q_ref[...]
    tri = tri_ref[...]
    t_loc = lax.broadcasted_iota(jnp.int32, (B_TQ, B_CAT), 0)
    s_loc = jnp.bitwise_and(lax.broadcasted_iota(jnp.int32, (B_TQ, B_CAT), 1), B_BLK - 1)

    acc_s[...] = jnp.zeros_like(acc_s)
    run_s[...] = jnp.zeros_like(run_s)

    def block(j, diag_offset):
        start = pl.multiple_of(j * B_CAT, B_CAT)
        zn = _dot_nt(q, kbd_ref[pl.ds(start, B_CAT), :])
        soft = jnp.log2(1.0 + jnp.exp2(-jnp.abs(zn)))
        log_stay = jnp.minimum(zn, 0.0) - soft
        if diag_offset is not None:
            causal = s_loc + diag_offset < t_loc
            log_stay = jnp.where(causal, log_stay, 0.0)
        ls16 = log_stay.astype(BF16)
        half = SB_HEADS // 2 * B_BLK
        sums = [_dot(ls16[:, p * half:(p + 1) * half], tri) for p in range(2)]
        later = jnp.concatenate([s[:, 0:half] for s in sums], axis=1)
        whole = jnp.concatenate([s[:, half:2 * half] for s in sums], axis=1)
        run = run_s[...]
        p = jnp.exp2((log_stay - zn) + later + run)
        if diag_offset is not None:
            p = jnp.where(causal, p, 0.0)
        run_s[...] = run + whole
        acc_s[...] += _dot(p.astype(BF16), vbd_ref[pl.ds(start, B_CAT), :])

    per_q = B_TQ // B_BLK
    for d in range(per_q - 1, -1, -1):
        block(i * per_q + d, d * B_BLK)

    def cond(c):
        jj, live = c
        return jnp.logical_and(jj < i, live)

    def body(c):
        jj, _ = c
        for d in range(per_q):
            block((i - jj) * per_q - 1 - d, None)
        run = run_s[...]
        top = run[:, 0:B_BLK]
        for h in range(1, SB_HEADS):
            top = jnp.maximum(top, run[:, h * B_BLK:(h + 1) * B_BLK])
        return jj + 1, jnp.max(top) > RUN_FLOOR

    lax.while_loop(cond, body, (jnp.int32(0), jnp.bool_(True)))
    o_ref[...] = acc_s[...].astype(o_ref.dtype)


def _mixer_b(qb, kbd, vbd, tri, batch, seq):
    t = qb.shape[0]
    nq = seq // B_TQ
    whole_seq = pl.BlockSpec((None, SB_HEADS * seq, W_B), lambda b, i: (b, 0, 0), pipeline_mode=pl.Buffered(1))
    out = pl.pallas_call(
        _mixer_b_kernel,
        grid=(batch, nq),
        in_specs=[
            pl.BlockSpec((None, B_TQ, W_B), lambda b, i: (b, i, 0)),
            whole_seq,
            whole_seq,
            pl.BlockSpec((2 * B_BLK, 4 * B_BLK), lambda b, i: (0, 0)),
        ],
        out_specs=pl.BlockSpec((None, B_TQ, W_B), lambda b, i: (b, i, 0)),
        out_shape=jax.ShapeDtypeStruct((batch, seq, W_B), BF16),
        scratch_shapes=[pltpu.VMEM((B_TQ, W_B), F32), pltpu.VMEM((B_TQ, B_CAT), F32)],
        compiler_params=pltpu.CompilerParams(
            dimension_semantics=("arbitrary", "arbitrary"), vmem_limit_bytes=VMEM_LIMIT),
        name="mixer_b",
    )(qb.reshape(batch, seq, W_B), kbd.reshape(batch, SB_HEADS * seq, W_B),
      vbd.reshape(batch, SB_HEADS * seq, W_B), tri)
    return out.reshape(t, W_B)


MERGE_TM = 512


def _merge_kernel(x_ref, oa_ref, ob_ref, pg_ref, bg_ref, wa_ref, wb_ref, wo_ref, o_ref):
    gates = jax.nn.sigmoid(pg_ref[...].astype(F32) + bg_ref[...])
    ya = _dot(oa_ref[...], wa_ref[...])
    yb = _dot(ob_ref[...], wb_ref[...])
    merged = gates[:, 0:D_MODEL] * ya + gates[:, D_MODEL:2 * D_MODEL] * yb
    o_ref[...] = x_ref[...] + _dot(merged.astype(BF16), wo_ref[...])


def _merge(x, oa, ob, pg, bg, wa, wb, wo):
    t = x.shape[0]
    row = lambda w: pl.BlockSpec((MERGE_TM, w), lambda i: (i, 0))
    full = lambda r, c: pl.BlockSpec((r, c), lambda i: (0, 0))
    return pl.pallas_call(
        _merge_kernel,
        grid=(t // MERGE_TM,),
        in_specs=[row(D_MODEL), row(OUT_A), row(W_B), row(2 * D_MODEL), full(1, 2 * D_MODEL),
                  full(OUT_A, D_MODEL), full(W_B, D_MODEL), full(D_MODEL, D_MODEL)],
        out_specs=row(D_MODEL),
        out_shape=jax.ShapeDtypeStruct((t, D_MODEL), F32),
        compiler_params=pltpu.CompilerParams(dimension_semantics=("arbitrary",), vmem_limit_bytes=VMEM_LIMIT),
        name="merge",
    )(x, oa, ob, pg, bg.reshape(1, 2 * D_MODEL), wa, wb, wo)


FFN_TM = 1024
FFN_TF = 256
HALO = 16


def _ffn_kernel(x_ref, xh_ref, g_ref, wup_ref, cw_ref, cb_ref, wd_ref, gf_ref, o_ref,
                h_s, up_s, acc_s, *, tiles_per_seq, final_norm):
    g = g_ref[...]
    h_s[0:HALO, :] = _rms(xh_ref[...], g).astype(BF16)
    h_s[HALO:HALO + FFN_TM, :] = _rms(x_ref[...], g).astype(BF16)
    seq_start = (pl.program_id(0) % tiles_per_seq) == 0

    for c in range(D_FF // FFN_TF):
        cols = slice(c * FFN_TF, (c + 1) * FFN_TF)
        buf = up_s.at[c % 2]
        up = _dot(h_s[...], wup_ref[:, cols])
        buf[0:HALO, :] = jnp.where(seq_start, 0.0, up[0:HALO])
        buf[HALO:HALO + FFN_TM, :] = up[HALO:HALO + FFN_TM]
        a = cb_ref[:, cols]
        for tap in range(CONV_WIDTH):
            lag = CONV_WIDTH - 1 - tap
            a = a + buf[HALO - lag:HALO - lag + FFN_TM, :] * cw_ref[tap:tap + 1, cols]
        val = _dot(h_s[HALO:HALO + FFN_TM, :], wup_ref[:, D_FF + c * FFN_TF:D_FF + (c + 1) * FFN_TF])
        gelu = 0.5 * a * (1.0 + lax.erf(a * (2.0 ** -0.5)))
        down = _dot((gelu * val).astype(BF16), wd_ref[cols, :])
        if c == 0:
            acc_s[...] = down
        else:
            acc_s[...] += down

    y = x_ref[...] + acc_s[...]
    if final_norm:
        y = _rms(y, gf_ref[...])
    o_ref[...] = y


def _ffn(x, g, w_up, cw, cb, w_down, gf, seq, final_norm):
    t = x.shape[0]
    halo_blocks = FFN_TM // HALO
    kern = functools.partial(_ffn_kernel, tiles_per_seq=seq // FFN_TM, final_norm=final_norm)
    resident = lambda r, c: pl.BlockSpec((r, c), lambda i: (0, 0), pipeline_mode=pl.Buffered(1))
    return pl.pallas_call(
        kern,
        grid=(t // FFN_TM,),
        in_specs=[
            pl.BlockSpec((FFN_TM, D_MODEL), lambda i: (i, 0)),
            pl.BlockSpec((HALO, D_MODEL), lambda i: (jnp.maximum(i * halo_blocks - 1, 0), 0)),
            resident(1, D_MODEL),
            resident(D_MODEL, 2 * D_FF),
            resident(CONV_WIDTH, D_FF),
            resident(1, D_FF),
            resident(D_FF, D_MODEL),
            resident(1, D_MODEL),
        ],
        out_specs=pl.BlockSpec((FFN_TM, D_MODEL), lambda i: (i, 0)),
        out_shape=jax.ShapeDtypeStruct((t, D_MODEL), F32),
        scratch_shapes=[
            pltpu.VMEM((HALO + FFN_TM, D_MODEL), BF16),
            pltpu.VMEM((2, HALO + FFN_TM, FFN_TF), F32),
            pltpu.VMEM((FFN_TM, D_MODEL), F32),
        ],
        compiler_params=pltpu.CompilerParams(dimension_semantics=("arbitrary",), vmem_limit_bytes=VMEM_LIMIT),
        name="ffn",
    )(x, x, g.reshape(1, D_MODEL), w_up, cw, cb.reshape(1, D_FF), w_down, gf.reshape(1, D_MODEL))


def kernel(x, norm1, w_in, b_gate, w_br, w_o, norm2, w_up, conv_w, conv_b, w_down, norm_f):
    batch, seq, d = x.shape
    depth = norm1.shape[0]
    assert d == D_MODEL and seq % A_TILE == 0 and seq % FFN_TM == 0
    t = batch * seq
    bias = jnp.asarray(_alibi_bias())
    tri = jnp.asarray(_suffix_sum_matrix(), dtype=BF16)
    xf = x.reshape(t, d)
    for l in range(depth):
        pa, qb, kbd, vbd, pg = _in_proj(xf, norm1[l], w_in[l].astype(BF16))
        oa = _mixer_a(pa, bias, batch, seq)
        ob = _mixer_b(qb, kbd, vbd, tri, batch, seq)
        wbr = w_br[l].astype(BF16)
        x1 = _merge(xf, oa, ob, pg, b_gate[l], wbr[:OUT_A], wbr[OUT_A:], w_o[l].astype(BF16))
        xf = _ffn(x1, norm2[l], w_up[l].astype(BF16), conv_w[l], conv_b[l], w_down[l].astype(BF16),
                  norm_f, seq, final_norm=(l == depth - 1))
    return xf.reshape(batch, seq, d)
```

```python
import functools

import numpy as np
import jax
import jax.numpy as jnp
from jax import lax
from jax.experimental import pallas as pl
from jax.experimental.pallas import tpu as pltpu

D_MODEL = 1024
HEAD_DIM = 64
DSW_GROUPS = ((128, 1), (512, 4), (2048, 16))
N_GROUPS = len(DSW_GROUPS)
HEADS_PER_GROUP = 4
DSW_HEADS = HEADS_PER_GROUP * N_GROUPS
SB_HEADS = 4
W_A = DSW_HEADS * HEAD_DIM
W_B = SB_HEADS * HEAD_DIM
OUT_A = HEADS_PER_GROUP * HEAD_DIM
N_IN = 3 * W_A + 3 * W_B + 2 * D_MODEL
D_FF = 2816
CONV_WIDTH = 3
RMS_EPS = 1e-6
QK_SCALE = HEAD_DIM ** -0.5

LANES = 128
WIN = 128
MASKED = -1e30

VMEM_LIMIT = 56 * 1024 * 1024

F32 = jnp.float32
BF16 = jnp.bfloat16


def _dot(a, b):
    return jnp.dot(a, b, preferred_element_type=F32)


def _dot_nt(a, b):
    return lax.dot_general(a, b, (((1,), (1,)), ((), ())), preferred_element_type=F32)


def _rms(x, g):
    ms = jnp.mean(x * x, axis=-1, keepdims=True)
    return x * lax.rsqrt(ms + RMS_EPS) * g


IN_TM = 512
IN_CHUNK = 256
B_BLK = 128
B_TQ = 256
B_CAT = SB_HEADS * B_BLK
LOG2E = 1.4426950408889634
RUN_FLOOR = -160.0


def _in_proj_kernel(x_ref, g_ref, w_ref, pa_ref, qb_ref, kbd_ref, vbd_ref, pg_ref):
    h = _rms(x_ref[...], g_ref[...]).astype(BF16)

    def proj(base):
        return _dot(h, w_ref[:, base:base + IN_CHUNK])

    for c in range(0, 3 * W_A, IN_CHUNK):
        pa_ref[:, c:c + IN_CHUNK] = proj(c)
    base_b = 3 * W_A
    qb_ref[...] = (proj(base_b) * (-QK_SCALE * LOG2E)).astype(BF16)
    lane = lax.broadcasted_iota(jnp.int32, (B_BLK, W_B), 1)
    for ref, base in ((kbd_ref, base_b + W_B), (vbd_ref, base_b + 2 * W_B)):
        val = proj(base)
        for r in range(IN_TM // B_BLK):
            rows = val[r * B_BLK:(r + 1) * B_BLK]
            for hd in range(SB_HEADS):
                keep = (lane >= hd * HEAD_DIM) & (lane < (hd + 1) * HEAD_DIM)
                ref[(r * SB_HEADS + hd) * B_BLK:(r * SB_HEADS + hd + 1) * B_BLK, :] = (
                    jnp.where(keep, rows, 0.0).astype(BF16))
    base_g = 3 * W_A + 3 * W_B
    for c in range(0, 2 * D_MODEL, IN_CHUNK):
        pg_ref[:, c:c + IN_CHUNK] = proj(base_g + c).astype(BF16)


def _in_proj(x, g, w):
    t = x.shape[0]
    return pl.pallas_call(
        _in_proj_kernel,
        grid=(t // IN_TM,),
        in_specs=[
            pl.BlockSpec((IN_TM, D_MODEL), lambda i: (i, 0)),
            pl.BlockSpec((1, D_MODEL), lambda i: (0, 0)),
            pl.BlockSpec((D_MODEL, N_IN), lambda i: (0, 0)),
        ],
        out_specs=[
            pl.BlockSpec((IN_TM, 3 * W_A), lambda i: (i, 0)),
            pl.BlockSpec((IN_TM, W_B), lambda i: (i, 0)),
            pl.BlockSpec((SB_HEADS * IN_TM, W_B), lambda i: (i, 0)),
            pl.BlockSpec((SB_HEADS * IN_TM, W_B), lambda i: (i, 0)),
            pl.BlockSpec((IN_TM, 2 * D_MODEL), lambda i: (i, 0)),
        ],
        out_shape=[
            jax.ShapeDtypeStruct((t, 3 * W_A), F32),
            jax.ShapeDtypeStruct((t, W_B), BF16),
            jax.ShapeDtypeStruct((SB_HEADS * t, W_B), BF16),
            jax.ShapeDtypeStruct((SB_HEADS * t, W_B), BF16),
            jax.ShapeDtypeStruct((t, 2 * D_MODEL), BF16),
        ],
        compiler_params=pltpu.CompilerParams(dimension_semantics=("arbitrary",), vmem_limit_bytes=VMEM_LIMIT),
        name="in_proj",
    )(x, g.reshape(1, D_MODEL), w)


A_TILE = WIN * max(d for _, d in DSW_GROUPS)


def _alibi_bias():
    slopes = 2.0 ** (-8.0 * np.arange(1, DSW_HEADS + 1) / DSW_HEADS)
    qi = np.arange(WIN)[:, None]
    kj = np.arange(2 * WIN)[None, :]
    delta = qi + WIN - kj
    valid = (delta >= 0) & (delta <= WIN)
    out = np.empty((N_GROUPS, 2, 2, 2 * WIN, 2 * WIN), np.float32)
    for g, (_, dil) in enumerate(DSW_GROUPS):
        for h in range(HEADS_PER_GROUP):
            b = -slopes[g * HEADS_PER_GROUP + h] * LOG2E * (delta * dil).astype(np.float32)
            rows = slice((h % 2) * WIN, (h % 2 + 1) * WIN)
            out[g, h // 2, 0, rows] = np.where(valid, b, MASKED)
            out[g, h // 2, 1, rows] = np.where(valid & (kj >= WIN), b, MASKED)
    return out


A_UNROLL = {1: 5, 4: 6, 16: 8}


def _mixer_a_kernel(*refs):
    q_refs = refs[0:3]
    kc_refs = refs[3:6]
    vc_refs = refs[6:9]
    kp_refs = refs[9:12]
    vp_refs = refs[12:15]
    bias_ref = refs[15]
    o_ref = refs[16]
    acc_s, m_s, l_s = refs[17:20]

    first = (pl.program_id(1) == 0).astype(jnp.int32)
    lane = lax.broadcasted_iota(jnp.int32, (WIN, LANES), 1)
    head0 = lane < HEAD_DIM
    ones = jnp.ones((2 * WIN, LANES), BF16)
    n_units = A_TILE // WIN

    for g, (_, dil) in enumerate(DSW_GROUPS):
        q_ref, kc_ref, vc_ref = q_refs[g], kc_refs[g], vc_refs[g]

        def rows(ref, start, dil=dil):
            if dil == 1:
                return ref[pl.ds(start, WIN), :]
            return ref[pl.ds(start, WIN, stride=dil), :]

        def unit(kprev_ref, vprev_ref, prev_off, off, sel, g=g, dil=dil, q_ref=q_ref, kc_ref=kc_ref,
                 vc_ref=vc_ref, rows=rows):
            q = rows(q_ref, off) * (QK_SCALE * LOG2E)
            q2 = jnp.concatenate([jnp.where(head0, q, 0.0), jnp.where(head0, 0.0, q)], axis=0).astype(BF16)
            k = jnp.concatenate([rows(kprev_ref, prev_off), rows(kc_ref, off)], axis=0).astype(BF16)
            v = jnp.concatenate([rows(vprev_ref, prev_off), rows(vc_ref, off)], axis=0).astype(BF16)
            s = _dot_nt(q2, k) + bias_ref[g, sel]
            m = jnp.max(s, axis=-1, keepdims=True)
            p = jnp.exp2(s - m).astype(BF16)
            pvl = _dot(p, jnp.concatenate([v, ones], axis=1))
            dst = pl.ds(off, WIN) if dil == 1 else pl.ds(off, WIN, stride=dil)
            acc_s[g, dst, :] = jnp.where(head0, pvl[0:WIN, 0:LANES], pvl[WIN:2 * WIN, 0:LANES])
            l_s[g, dst, :] = jnp.where(head0, pvl[0:WIN, LANES:2 * LANES], pvl[WIN:2 * WIN, LANES:2 * LANES])
            m_s[g, dst, :] = jnp.where(head0, m[0:WIN], m[WIN:2 * WIN])

        def head_unit(r, unit=unit, kp_ref=kp_refs[g], vp_ref=vp_refs[g]):
            unit(kp_ref, vp_ref, r, r, first)

        def tail_unit(idx, unit=unit, kc_ref=kc_ref, vc_ref=vc_ref, dil=dil):
            shift = dil.bit_length() - 1
            u = idx >> shift if isinstance(idx, int) else lax.shift_right_logical(idx, shift)
            off = u * (WIN * dil) + (idx & (dil - 1))
            unit(kc_ref, vc_ref, off - WIN * dil, off, 0)

        unroll = A_UNROLL[dil]
        for fn, lo, hi in ((head_unit, 0, dil), (tail_unit, dil, n_units)):
            count = hi - lo
            if count == 0:
                continue
            if count <= unroll:
                for idx in range(lo, hi):
                    fn(idx)
                continue
            assert count % unroll == 0

            def body(t, c, fn=fn, lo=lo, unroll=unroll):
                for uu in range(unroll):
                    fn(lo + t * unroll + uu)
                return c

            lax.fori_loop(0, count // unroll, body, 0)

    m_all = jnp.maximum(jnp.maximum(m_s[0], m_s[1]), m_s[2])
    num = jnp.zeros((A_TILE, LANES), F32)
    den = jnp.zeros((A_TILE, LANES), F32)
    for g in range(N_GROUPS):
        w = jnp.exp2(m_s[g] - m_all)
        num = num + w * acc_s[g]
        den = den + w * l_s[g]
    o_ref[...] = (num / den).astype(o_ref.dtype)


def _mixer_a(pa, bias, batch, seq):
    t = pa.shape[0]
    tiles = seq // A_TILE
    qcol = lambda g: (lambda b, n, hp, g=g: (b * tiles + n, 0 * 6 + 2 * g + hp))
    kcol = lambda g: (lambda b, n, hp, g=g: (b * tiles + n, 1 * 6 + 2 * g + hp))
    vcol = lambda g: (lambda b, n, hp, g=g: (b * tiles + n, 2 * 6 + 2 * g + hp))

    def prev(kind, g):
        per_tile = A_TILE // (WIN * DSW_GROUPS[g][1])
        return lambda b, n, hp: (jnp.maximum((b * tiles + n) * per_tile - 1, 0), kind * 6 + 2 * g + hp)

    in_specs = (
        [pl.BlockSpec((A_TILE, LANES), qcol(g)) for g in range(N_GROUPS)]
        + [pl.BlockSpec((A_TILE, LANES), kcol(g)) for g in range(N_GROUPS)]
        + [pl.BlockSpec((A_TILE, LANES), vcol(g)) for g in range(N_GROUPS)]
        + [pl.BlockSpec((WIN * DSW_GROUPS[g][1], LANES), prev(1, g)) for g in range(N_GROUPS)]
        + [pl.BlockSpec((WIN * DSW_GROUPS[g][1], LANES), prev(2, g)) for g in range(N_GROUPS)]
        + [pl.BlockSpec((N_GROUPS, None, 2, 2 * WIN, 2 * WIN), lambda b, n, hp: (0, hp, 0, 0, 0))]
    )
    return pl.pallas_call(
        _mixer_a_kernel,
        grid=(batch, tiles, 2),
        in_specs=in_specs,
        out_specs=pl.BlockSpec((A_TILE, LANES), lambda b, n, hp: (b * tiles + n, hp)),
        out_shape=jax.ShapeDtypeStruct((t, OUT_A), BF16),
        scratch_shapes=[pltpu.VMEM((N_GROUPS, A_TILE, LANES), F32)] * 3,
        compiler_params=pltpu.CompilerParams(
            dimension_semantics=("arbitrary", "arbitrary", "arbitrary"), vmem_limit_bytes=VMEM_LIMIT),
        name="mixer_a",
    )(*([pa] * 15), bias)


def _suffix_sum_matrix():
    j = np.arange(B_BLK)[:, None]
    s = np.arange(B_BLK)[None, :]
    later = (j > s).astype(np.float32)
    ones = np.ones((B_BLK, B_BLK), np.float32)
    zero = np.zeros((B_BLK, B_BLK), np.float32)
    return np.block([[later, zero, ones, zero], [zero, later, zero, ones]])


def _mixer_b_kernel(q_ref, kbd_ref, vbd_ref, tri_ref, o_ref, acc_s, run_s):
    i = pl.program_id(1)
    q = q_ref[...]
    tri = tri_ref[...]
    t_loc = lax.broadcasted_iota(jnp.int32, (B_TQ, B_CAT), 0)
    s_loc = jnp.bitwise_and(lax.broadcasted_iota(jnp.int32, (B_TQ, B_CAT), 1), B_BLK - 1)

    acc_s[...] = jnp.zeros_like(acc_s)
    run_s[...] = jnp.zeros_like(run_s)

    def block(j, diag_offset):
        start = pl.multiple_of(j * B_CAT, B_CAT)
        zn = _dot_nt(q, kbd_ref[pl.ds(start, B_CAT), :])
        soft = jnp.log2(1.0 + jnp.exp2(-jnp.abs(zn)))
        log_stay = jnp.minimum(zn, 0.0) - soft
        if diag_offset is not None:
            causal = s_loc + diag_offset < t_loc
            log_stay = jnp.where(causal, log_stay, 0.0)
        ls16 = log_stay.astype(BF16)
        half = SB_HEADS // 2 * B_BLK
        sums = [_dot(ls16[:, p * half:(p + 1) * half], tri) for p in range(2)]
        later = jnp.concatenate([s[:, 0:half] for s in sums], axis=1)
        whole = jnp.concatenate([s[:, half:2 * half] for s in sums], axis=1)
        run = run_s[...]
        p = jnp.exp2((log_stay - zn) + later + run)
        if diag_offset is not None:
            p = jnp.where(causal, p, 0.0)
        run_s[...] = run + whole
        acc_s[...] += _dot(p.astype(BF16), vbd_ref[pl.ds(start, B_CAT), :])

    per_q = B_TQ // B_BLK
    for d in range(per_q - 1, -1, -1):
        block(i * per_q + d, d * B_BLK)

    def cond(c):
        jj, live = c
        return jnp.logical_and(jj < i, live)

    def body(c):
        jj, _ = c
        for d in range(per_q):
            block((i - jj) * per_q - 1 - d, None)
        run = run_s[...]
        top = run[:, 0:B_BLK]
        for h in range(1, SB_HEADS):
            top = jnp.maximum(top, run[:, h * B_BLK:(h + 1) * B_BLK])
        return jj + 1, jnp.max(top) > RUN_FLOOR

    lax.while_loop(cond, body, (jnp.int32(0), jnp.bool_(True)))
    o_ref[...] = acc_s[...].astype(o_ref.dtype)


def _mixer_b(qb, kbd, vbd, tri, batch, seq):
    t = qb.shape[0]
    nq = seq // B_TQ
    whole_seq = pl.BlockSpec((None, SB_HEADS * seq, W_B), lambda b, i: (b, 0, 0), pipeline_mode=pl.Buffered(1))
    out = pl.pallas_call(
        _mixer_b_kernel,
        grid=(batch, nq),
        in_specs=[
            pl.BlockSpec((None, B_TQ, W_B), lambda b, i: (b, i, 0)),
            whole_seq,
            whole_seq,
            pl.BlockSpec((2 * B_BLK, 4 * B_BLK), lambda b, i: (0, 0)),
        ],
        out_specs=pl.BlockSpec((None, B_TQ, W_B), lambda b, i: (b, i, 0)),
        out_shape=jax.ShapeDtypeStruct((batch, seq, W_B), BF16),
        scratch_shapes=[pltpu.VMEM((B_TQ, W_B), F32), pltpu.VMEM((B_TQ, B_CAT), F32)],
        compiler_params=pltpu.CompilerParams(
            dimension_semantics=("arbitrary", "arbitrary"), vmem_limit_bytes=VMEM_LIMIT),
        name="mixer_b",
    )(qb.reshape(batch, seq, W_B), kbd.reshape(batch, SB_HEADS * seq, W_B),
      vbd.reshape(batch, SB_HEADS * seq, W_B), tri)
    return out.reshape(t, W_B)


MERGE_TM = 512


def _merge_kernel(x_ref, oa_ref, ob_ref, pg_ref, bg_ref, wa_ref, wb_ref, wo_ref, o_ref):
    gates = jax.nn.sigmoid(pg_ref[...].astype(F32) + bg_ref[...])
    ya = _dot(oa_ref[...], wa_ref[...])
    yb = _dot(ob_ref[...], wb_ref[...])
    merged = gates[:, 0:D_MODEL] * ya + gates[:, D_MODEL:2 * D_MODEL] * yb
    o_ref[...] = x_ref[...] + _dot(merged.astype(BF16), wo_ref[...])


def _merge(x, oa, ob, pg, bg, wa, wb, wo):
    t = x.shape[0]
    row = lambda w: pl.BlockSpec((MERGE_TM, w), lambda i: (i, 0))
    full = lambda r, c: pl.BlockSpec((r, c), lambda i: (0, 0))
    return pl.pallas_call(
        _merge_kernel,
        grid=(t // MERGE_TM,),
        in_specs=[row(D_MODEL), row(OUT_A), row(W_B), row(2 * D_MODEL), full(1, 2 * D_MODEL),
                  full(OUT_A, D_MODEL), full(W_B, D_MODEL), full(D_MODEL, D_MODEL)],
        out_specs=row(D_MODEL),
        out_shape=jax.ShapeDtypeStruct((t, D_MODEL), F32),
        compiler_params=pltpu.CompilerParams(dimension_semantics=("arbitrary",), vmem_limit_bytes=VMEM_LIMIT),
        name="merge",
    )(x, oa, ob, pg, bg.reshape(1, 2 * D_MODEL), wa, wb, wo)


FFN_TM = 1024
FFN_TF = 256
HALO = 16


def _ffn_kernel(x_ref, xh_ref, g_ref, wup_ref, cw_ref, cb_ref, wd_ref, gf_ref, o_ref,
                h_s, up_s, val_s, acc_s, *, tiles_per_seq, final_norm):
    g = g_ref[...]
    h_s[0:HALO, :] = _rms(xh_ref[...], g).astype(BF16)
    h_s[HALO:HALO + FFN_TM, :] = _rms(x_ref[...], g).astype(BF16)
    seq_start = (pl.program_id(0) % tiles_per_seq) == 0
    n_chunks = D_FF // FFN_TF

    def project(c):
        slot = c % 2
        up = _dot(h_s[...], wup_ref[:, c * FFN_TF:(c + 1) * FFN_TF])
        up_s[slot, 0:HALO, :] = jnp.where(seq_start, 0.0, up[0:HALO])
        up_s[slot, HALO:HALO + FFN_TM, :] = up[HALO:HALO + FFN_TM]
        val_s[slot] = _dot(h_s[HALO:HALO + FFN_TM, :], wup_ref[:, D_FF + c * FFN_TF:D_FF + (c + 1) * FFN_TF])

    project(0)
    for c in range(n_chunks):
        if c + 1 < n_chunks:
            project(c + 1)
        slot = c % 2
        cols = slice(c * FFN_TF, (c + 1) * FFN_TF)
        a = cb_ref[:, cols]
        for tap in range(CONV_WIDTH):
            lag = CONV_WIDTH - 1 - tap
            a = a + up_s[slot, HALO - lag:HALO - lag + FFN_TM, :] * cw_ref[tap:tap + 1, cols]
        gelu = 0.5 * a * (1.0 + lax.erf(a * (2.0 ** -0.5)))
        down = _dot((gelu * val_s[slot]).astype(BF16), wd_ref[cols, :])
        if c == 0:
            acc_s[...] = down
        else:
            acc_s[...] += down

    y = x_ref[...] + acc_s[...]
    if final_norm:
        y = _rms(y, gf_ref[...])
    o_ref[...] = y


def _ffn(x, g, w_up, cw, cb, w_down, gf, seq, final_norm):
    t = x.shape[0]
    halo_blocks = FFN_TM // HALO
    kern = functools.partial(_ffn_kernel, tiles_per_seq=seq // FFN_TM, final_norm=final_norm)
    resident = lambda r, c: pl.BlockSpec((r, c), lambda i: (0, 0), pipeline_mode=pl.Buffered(1))
    return pl.pallas_call(
        kern,
        grid=(t // FFN_TM,),
        in_specs=[
            pl.BlockSpec((FFN_TM, D_MODEL), lambda i: (i, 0)),
            pl.BlockSpec((HALO, D_MODEL), lambda i: (jnp.maximum(i * halo_blocks - 1, 0), 0)),
            resident(1, D_MODEL),
            resident(D_MODEL, 2 * D_FF),
            resident(CONV_WIDTH, D_FF),
            resident(1, D_FF),
            resident(D_FF, D_MODEL),
            resident(1, D_MODEL),
        ],
        out_specs=pl.BlockSpec((FFN_TM, D_MODEL), lambda i: (i, 0)),
        out_shape=jax.ShapeDtypeStruct((t, D_MODEL), F32),
        scratch_shapes=[
            pltpu.VMEM((HALO + FFN_TM, D_MODEL), BF16),
            pltpu.VMEM((2, HALO + FFN_TM, FFN_TF), F32),
            pltpu.VMEM((2, FFN_TM, FFN_TF), F32),
            pltpu.VMEM((FFN_TM, D_MODEL), F32),
        ],
        compiler_params=pltpu.CompilerParams(dimension_semantics=("arbitrary",), vmem_limit_bytes=VMEM_LIMIT),
        name="ffn",
    )(x, x, g.reshape(1, D_MODEL), w_up, cw, cb.reshape(1, D_FF), w_down, gf.reshape(1, D_MODEL))


def kernel(x, norm1, w_in, b_gate, w_br, w_o, norm2, w_up, conv_w, conv_b, w_down, norm_f):
    batch, seq, d = x.shape
    depth = norm1.shape[0]
    assert d == D_MODEL and seq % A_TILE == 0 and seq % FFN_TM == 0
    t = batch * seq
    bias = jnp.asarray(_alibi_bias())
    tri = jnp.asarray(_suffix_sum_matrix(), dtype=BF16)
    xf = x.reshape(t, d)
    for l in range(depth):
        pa, qb, kbd, vbd, pg = _in_proj(xf, norm1[l], w_in[l].astype(BF16))
        oa = _mixer_a(pa, bias, batch, seq)
        ob = _mixer_b(qb, kbd, vbd, tri, batch, seq)
        wbr = w_br[l].astype(BF16)
        x1 = _merge(xf, oa, ob, pg, b_gate[l], wbr[:OUT_A], wbr[OUT_A:], w_o[l].astype(BF16))
        xf = _ffn(x1, norm2[l], w_up[l].astype(BF16), conv_w[l], conv_b[l], w_down[l].astype(BF16),
                  norm_f, seq, final_norm=(l == depth - 1))
    return xf.reshape(batch, seq, d)
```

```python
import functools

import numpy as np
import jax
import jax.numpy as jnp
from jax import lax
from jax.experimental import pallas as pl
from jax.experimental.pallas import tpu as pltpu

D_MODEL = 1024
HEAD_DIM = 64
DSW_GROUPS = ((128, 1), (512, 4), (2048, 16))
N_GROUPS = len(DSW_GROUPS)
HEADS_PER_GROUP = 4
DSW_HEADS = HEADS_PER_GROUP * N_GROUPS
SB_HEADS = 4
W_A = DSW_HEADS * HEAD_DIM
W_B = SB_HEADS * HEAD_DIM
OUT_A = HEADS_PER_GROUP * HEAD_DIM
N_IN = 3 * W_A + 3 * W_B + 2 * D_MODEL
D_FF = 2816
CONV_WIDTH = 3
RMS_EPS = 1e-6
QK_SCALE = HEAD_DIM ** -0.5

LANES = 128
WIN = 128
MASKED = -1e30

VMEM_LIMIT = 56 * 1024 * 1024

F32 = jnp.float32
BF16 = jnp.bfloat16


def _dot(a, b):
    return jnp.dot(a, b, preferred_element_type=F32)


def _dot_nt(a, b):
    return lax.dot_general(a, b, (((1,), (1,)), ((), ())), preferred_element_type=F32)


def _rms(x, g):
    ms = jnp.mean(x * x, axis=-1, keepdims=True)
    return x * lax.rsqrt(ms + RMS_EPS) * g


IN_TM = 512
IN_CHUNK = 256
B_BLK = 128
B_TQ = 256
B_STREAMS = 4
B_CAT = SB_HEADS * B_BLK
LOG2E = 1.4426950408889634
RUN_FLOOR = -160.0


def _in_proj_kernel(x_ref, g_ref, w_ref, pa_ref, qb_ref, kbd_ref, vbd_ref, pg_ref):
    h = _rms(x_ref[...], g_ref[...]).astype(BF16)

    def proj(base):
        return _dot(h, w_ref[:, base:base + IN_CHUNK])

    for c in range(0, 3 * W_A, IN_CHUNK):
        pa_ref[:, c:c + IN_CHUNK] = proj(c)
    base_b = 3 * W_A
    qb_ref[...] = (proj(base_b) * (-QK_SCALE * LOG2E)).astype(BF16)
    lane = lax.broadcasted_iota(jnp.int32, (B_BLK, W_B), 1)
    for ref, base in ((kbd_ref, base_b + W_B), (vbd_ref, base_b + 2 * W_B)):
        val = proj(base)
        for r in range(IN_TM // B_BLK):
            rows = val[r * B_BLK:(r + 1) * B_BLK]
            for hd in range(SB_HEADS):
                keep = (lane >= hd * HEAD_DIM) & (lane < (hd + 1) * HEAD_DIM)
                ref[(r * SB_HEADS + hd) * B_BLK:(r * SB_HEADS + hd + 1) * B_BLK, :] = (
                    jnp.where(keep, rows, 0.0).astype(BF16))
    base_g = 3 * W_A + 3 * W_B
    for c in range(0, 2 * D_MODEL, IN_CHUNK):
        pg_ref[:, c:c + IN_CHUNK] = proj(base_g + c).astype(BF16)


def _in_proj(x, g, w):
    t = x.shape[0]
    return pl.pallas_call(
        _in_proj_kernel,
        grid=(t // IN_TM,),
        in_specs=[
            pl.BlockSpec((IN_TM, D_MODEL), lambda i: (i, 0)),
            pl.BlockSpec((1, D_MODEL), lambda i: (0, 0)),
            pl.BlockSpec((D_MODEL, N_IN), lambda i: (0, 0)),
        ],
        out_specs=[
            pl.BlockSpec((IN_TM, 3 * W_A), lambda i: (i, 0)),
            pl.BlockSpec((IN_TM, W_B), lambda i: (i, 0)),
            pl.BlockSpec((SB_HEADS * IN_TM, W_B), lambda i: (i, 0)),
            pl.BlockSpec((SB_HEADS * IN_TM, W_B), lambda i: (i, 0)),
            pl.BlockSpec((IN_TM, 2 * D_MODEL), lambda i: (i, 0)),
        ],
        out_shape=[
            jax.ShapeDtypeStruct((t, 3 * W_A), F32),
            jax.ShapeDtypeStruct((t, W_B), BF16),
            jax.ShapeDtypeStruct((SB_HEADS * t, W_B), BF16),
            jax.ShapeDtypeStruct((SB_HEADS * t, W_B), BF16),
            jax.ShapeDtypeStruct((t, 2 * D_MODEL), BF16),
        ],
        compiler_params=pltpu.CompilerParams(dimension_semantics=("arbitrary",), vmem_limit_bytes=VMEM_LIMIT),
        name="in_proj",
    )(x, g.reshape(1, D_MODEL), w)


A_TILE = WIN * max(d for _, d in DSW_GROUPS)


def _alibi_bias():
    slopes = 2.0 ** (-8.0 * np.arange(1, DSW_HEADS + 1) / DSW_HEADS)
    qi = np.arange(WIN)[:, None]
    kj = np.arange(2 * WIN)[None, :]
    delta = qi + WIN - kj
    valid = (delta >= 0) & (delta <= WIN)
    out = np.empty((N_GROUPS, 2, 2, 2 * WIN, 2 * WIN), np.float32)
    for g, (_, dil) in enumerate(DSW_GROUPS):
        for h in range(HEADS_PER_GROUP):
            b = -slopes[g * HEADS_PER_GROUP + h] * LOG2E * (delta * dil).astype(np.float32)
            rows = slice((h % 2) * WIN, (h % 2 + 1) * WIN)
            out[g, h // 2, 0, rows] = np.where(valid, b, MASKED)
            out[g, h // 2, 1, rows] = np.where(valid & (kj >= WIN), b, MASKED)
    return out


A_UNROLL = {1: 5, 4: 6, 16: 8}


def _mixer_a_kernel(*refs):
    q_refs = refs[0:3]
    kc_refs = refs[3:6]
    vc_refs = refs[6:9]
    kp_refs = refs[9:12]
    vp_refs = refs[12:15]
    bias_ref = refs[15]
    o_ref = refs[16]
    acc_s, m_s, l_s = refs[17:20]

    first = (pl.program_id(1) == 0).astype(jnp.int32)
    lane = lax.broadcasted_iota(jnp.int32, (WIN, LANES), 1)
    head0 = lane < HEAD_DIM
    ones = jnp.ones((2 * WIN, LANES), BF16)
    n_units = A_TILE // WIN

    for g, (_, dil) in enumerate(DSW_GROUPS):
        q_ref, kc_ref, vc_ref = q_refs[g], kc_refs[g], vc_refs[g]

        def rows(ref, start, dil=dil):
            if dil == 1:
                return ref[pl.ds(start, WIN), :]
            return ref[pl.ds(start, WIN, stride=dil), :]

        def unit(kprev_ref, vprev_ref, prev_off, off, sel, g=g, dil=dil, q_ref=q_ref, kc_ref=kc_ref,
                 vc_ref=vc_ref, rows=rows):
            q = rows(q_ref, off) * (QK_SCALE * LOG2E)
            q2 = jnp.concatenate([jnp.where(head0, q, 0.0), jnp.where(head0, 0.0, q)], axis=0).astype(BF16)
            k = jnp.concatenate([rows(kprev_ref, prev_off), rows(kc_ref, off)], axis=0).astype(BF16)
            v = jnp.concatenate([rows(vprev_ref, prev_off), rows(vc_ref, off)], axis=0).astype(BF16)
            s = _dot_nt(q2, k) + bias_ref[g, sel]
            m = jnp.max(s, axis=-1, keepdims=True)
            p = jnp.exp2(s - m).astype(BF16)
            pvl = _dot(p, jnp.concatenate([v, ones], axis=1))
            dst = pl.ds(off, WIN) if dil == 1 else pl.ds(off, WIN, stride=dil)
            acc_s[g, dst, :] = jnp.where(head0, pvl[0:WIN, 0:LANES], pvl[WIN:2 * WIN, 0:LANES])
            l_s[g, dst, :] = jnp.where(head0, pvl[0:WIN, LANES:2 * LANES], pvl[WIN:2 * WIN, LANES:2 * LANES])
            m_s[g, dst, :] = jnp.where(head0, m[0:WIN], m[WIN:2 * WIN])

        def head_unit(r, unit=unit, kp_ref=kp_refs[g], vp_ref=vp_refs[g]):
            unit(kp_ref, vp_ref, r, r, first)

        def tail_unit(idx, unit=unit, kc_ref=kc_ref, vc_ref=vc_ref, dil=dil):
            shift = dil.bit_length() - 1
            u = idx >> shift if isinstance(idx, int) else lax.shift_right_logical(idx, shift)
            off = u * (WIN * dil) + (idx & (dil - 1))
            unit(kc_ref, vc_ref, off - WIN * dil, off, 0)

        unroll = A_UNROLL[dil]
        for fn, lo, hi in ((head_unit, 0, dil), (tail_unit, dil, n_units)):
            count = hi - lo
            if count == 0:
                continue
            if count <= unroll:
                for idx in range(lo, hi):
                    fn(idx)
                continue
            assert count % unroll == 0

            def body(t, c, fn=fn, lo=lo, unroll=unroll):
                for uu in range(unroll):
                    fn(lo + t * unroll + uu)
                return c

            lax.fori_loop(0, count // unroll, body, 0)

    m_all = jnp.maximum(jnp.maximum(m_s[0], m_s[1]), m_s[2])
    num = jnp.zeros((A_TILE, LANES), F32)
    den = jnp.zeros((A_TILE, LANES), F32)
    for g in range(N_GROUPS):
        w = jnp.exp2(m_s[g] - m_all)
        num = num + w * acc_s[g]
        den = den + w * l_s[g]
    o_ref[...] = (num / den).astype(o_ref.dtype)


def _mixer_a(pa, bias, batch, seq):
    t = pa.shape[0]
    tiles = seq // A_TILE
    qcol = lambda g: (lambda b, n, hp, g=g: (b * tiles + n, 0 * 6 + 2 * g + hp))
    kcol = lambda g: (lambda b, n, hp, g=g: (b * tiles + n, 1 * 6 + 2 * g + hp))
    vcol = lambda g: (lambda b, n, hp, g=g: (b * tiles + n, 2 * 6 + 2 * g + hp))

    def prev(kind, g):
        per_tile = A_TILE // (WIN * DSW_GROUPS[g][1])
        return lambda b, n, hp: (jnp.maximum((b * tiles + n) * per_tile - 1, 0), kind * 6 + 2 * g + hp)

    in_specs = (
        [pl.BlockSpec((A_TILE, LANES), qcol(g)) for g in range(N_GROUPS)]
        + [pl.BlockSpec((A_TILE, LANES), kcol(g)) for g in range(N_GROUPS)]
        + [pl.BlockSpec((A_TILE, LANES), vcol(g)) for g in range(N_GROUPS)]
        + [pl.BlockSpec((WIN * DSW_GROUPS[g][1], LANES), prev(1, g)) for g in range(N_GROUPS)]
        + [pl.BlockSpec((WIN * DSW_GROUPS[g][1], LANES), prev(2, g)) for g in range(N_GROUPS)]
        + [pl.BlockSpec((N_GROUPS, None, 2, 2 * WIN, 2 * WIN), lambda b, n, hp: (0, hp, 0, 0, 0))]
    )
    return pl.pallas_call(
        _mixer_a_kernel,
        grid=(batch, tiles, 2),
        in_specs=in_specs,
        out_specs=pl.BlockSpec((A_TILE, LANES), lambda b, n, hp: (b * tiles + n, hp)),
        out_shape=jax.ShapeDtypeStruct((t, OUT_A), BF16),
        scratch_shapes=[pltpu.VMEM((N_GROUPS, A_TILE, LANES), F32)] * 3,
        compiler_params=pltpu.CompilerParams(
            dimension_semantics=("arbitrary", "arbitrary", "arbitrary"), vmem_limit_bytes=VMEM_LIMIT),
        name="mixer_a",
    )(*([pa] * 15), bias)


def _suffix_sum_matrix():
    j = np.arange(B_BLK)[:, None]
    s = np.arange(B_BLK)[None, :]
    later = (j > s).astype(np.float32)
    ones = np.ones((B_BLK, B_BLK), np.float32)
    zero = np.zeros((B_BLK, B_BLK), np.float32)
    return np.block([[later, zero, ones, zero], [zero, later, zero, ones]])


def _mixer_b_kernel(q_ref, kbd_ref, vbd_ref, tri_ref, o_ref, acc_s, run_s):
    step = pl.program_id(1)
    tri = tri_ref[...]
    t_loc = lax.broadcasted_iota(jnp.int32, (B_TQ, B_CAT), 0)
    s_loc = jnp.bitwise_and(lax.broadcasted_iota(jnp.int32, (B_TQ, B_CAT), 1), B_BLK - 1)
    per_q = B_TQ // B_BLK

    def block(s, j, diag_offset, first):
        start = pl.multiple_of(j * B_CAT, B_CAT)
        zn = _dot_nt(q_ref[s * B_TQ:(s + 1) * B_TQ, :], kbd_ref[pl.ds(start, B_CAT), :])
        soft = jnp.log2(1.0 + jnp.exp2(-jnp.abs(zn)))
        log_stay = jnp.minimum(zn, 0.0) - soft
        if diag_offset is not None:
            causal = s_loc + diag_offset < t_loc
            log_stay = jnp.where(causal, log_stay, 0.0)
        ls16 = log_stay.astype(BF16)
        half = SB_HEADS // 2 * B_BLK
        sums = [_dot(ls16[:, p * half:(p + 1) * half], tri) for p in range(2)]
        later = jnp.concatenate([x[:, 0:half] for x in sums], axis=1)
        whole = jnp.concatenate([x[:, half:2 * half] for x in sums], axis=1)
        expo = (log_stay - zn) + later
        if not first:
            run = run_s[s]
            expo = expo + run
            whole = whole + run
        p = jnp.exp2(expo)
        if diag_offset is not None:
            p = jnp.where(causal, p, 0.0)
        run_s[s] = whole
        pv = _dot(p.astype(BF16), vbd_ref[pl.ds(start, B_CAT), :])
        if first:
            acc_s[s] = pv
        else:
            acc_s[s] += pv

    def diagonal(s):
        i = step * B_STREAMS + s
        for d in range(per_q - 1, -1, -1):
            block(s, i * per_q + d, d * B_BLK, first=(d == per_q - 1))

    def trip(s, jj):
        i = step * B_STREAMS + s
        for d in range(per_q):
            block(s, (i - jj) * per_q - 1 - d, None, first=False)

    @pl.when(step > 0)
    def _():
        for s in range(B_STREAMS):
            diagonal(s)
            trip(s, 0)

    @pl.when(step == 0)
    def _():
        diagonal(0)
        for s in range(1, B_STREAMS):
            diagonal(s)
            trip(s, 0)

    def alive(s):
        run = run_s[s]
        top = run[:, 0:B_BLK]
        for h in range(1, SB_HEADS):
            top = jnp.maximum(top, run[:, h * B_BLK:(h + 1) * B_BLK])
        return jnp.max(top) > RUN_FLOOR

    for s in range(B_STREAMS):
        i = step * B_STREAMS + s

        def cond(c, i=i):
            jj, live = c
            return jnp.logical_and(jj < i, live)

        def body(c, s=s):
            jj, _ = c
            trip(s, jj)
            return jj + 1, alive(s)

        lax.while_loop(cond, body, (jnp.int32(1), alive(s)))

    o_ref[...] = acc_s[...].reshape(B_STREAMS * B_TQ, W_B).astype(o_ref.dtype)


def _mixer_b(qb, kbd, vbd, tri, batch, seq):
    t = qb.shape[0]
    rows = B_STREAMS * B_TQ
    whole_seq = pl.BlockSpec((None, SB_HEADS * seq, W_B), lambda b, i: (b, 0, 0), pipeline_mode=pl.Buffered(1))
    out = pl.pallas_call(
        _mixer_b_kernel,
        grid=(batch, seq // rows),
        in_specs=[
            pl.BlockSpec((None, rows, W_B), lambda b, i: (b, i, 0)),
            whole_seq,
            whole_seq,
            pl.BlockSpec((2 * B_BLK, 4 * B_BLK), lambda b, i: (0, 0)),
        ],
        out_specs=pl.BlockSpec((None, rows, W_B), lambda b, i: (b, i, 0)),
        out_shape=jax.ShapeDtypeStruct((batch, seq, W_B), BF16),
        scratch_shapes=[pltpu.VMEM((B_STREAMS, B_TQ, W_B), F32), pltpu.VMEM((B_STREAMS, B_TQ, B_CAT), F32)],
        compiler_params=pltpu.CompilerParams(
            dimension_semantics=("arbitrary", "arbitrary"), vmem_limit_bytes=VMEM_LIMIT),
        name="mixer_b",
    )(qb.reshape(batch, seq, W_B), kbd.reshape(batch, SB_HEADS * seq, W_B),
      vbd.reshape(batch, SB_HEADS * seq, W_B), tri)
    return out.reshape(t, W_B)


MERGE_TM = 512


def _merge_kernel(x_ref, oa_ref, ob_ref, pg_ref, bg_ref, wa_ref, wb_ref, wo_ref, o_ref):
    gates = jax.nn.sigmoid(pg_ref[...].astype(F32) + bg_ref[...])
    ya = _dot(oa_ref[...], wa_ref[...])
    yb = _dot(ob_ref[...], wb_ref[...])
    merged = gates[:, 0:D_MODEL] * ya + gates[:, D_MODEL:2 * D_MODEL] * yb
    o_ref[...] = x_ref[...] + _dot(merged.astype(BF16), wo_ref[...])


def _merge(x, oa, ob, pg, bg, wa, wb, wo):
    t = x.shape[0]
    row = lambda w: pl.BlockSpec((MERGE_TM, w), lambda i: (i, 0))
    full = lambda r, c: pl.BlockSpec((r, c), lambda i: (0, 0))
    return pl.pallas_call(
        _merge_kernel,
        grid=(t // MERGE_TM,),
        in_specs=[row(D_MODEL), row(OUT_A), row(W_B), row(2 * D_MODEL), full(1, 2 * D_MODEL),
                  full(OUT_A, D_MODEL), full(W_B, D_MODEL), full(D_MODEL, D_MODEL)],
        out_specs=row(D_MODEL),
        out_shape=jax.ShapeDtypeStruct((t, D_MODEL), F32),
        compiler_params=pltpu.CompilerParams(dimension_semantics=("arbitrary",), vmem_limit_bytes=VMEM_LIMIT),
        name="merge",
    )(x, oa, ob, pg, bg.reshape(1, 2 * D_MODEL), wa, wb, wo)


FFN_TM = 1024
FFN_TF = 256
HALO = 16


def _ffn_kernel(x_ref, xh_ref, g_ref, wup_ref, cw_ref, cb_ref, wd_ref, gf_ref, o_ref,
                h_s, up_s, val_s, acc_s, *, tiles_per_seq, final_norm):
    g = g_ref[...]
    h_s[0:HALO, :] = _rms(xh_ref[...], g).astype(BF16)
    h_s[HALO:HALO + FFN_TM, :] = _rms(x_ref[...], g).astype(BF16)
    seq_start = (pl.program_id(0) % tiles_per_seq) == 0
    n_chunks = D_FF // FFN_TF

    def project(c):
        slot = c % 2
        up = _dot(h_s[...], wup_ref[:, c * FFN_TF:(c + 1) * FFN_TF])
        up_s[slot, 0:HALO, :] = jnp.where(seq_start, 0.0, up[0:HALO])
        up_s[slot, HALO:HALO + FFN_TM, :] = up[HALO:HALO + FFN_TM]
        val_s[slot] = _dot(h_s[HALO:HALO + FFN_TM, :], wup_ref[:, D_FF + c * FFN_TF:D_FF + (c + 1) * FFN_TF])

    project(0)
    for c in range(n_chunks):
        if c + 1 < n_chunks:
            project(c + 1)
        slot = c % 2
        cols = slice(c * FFN_TF, (c + 1) * FFN_TF)
        a = cb_ref[:, cols]
        for tap in range(CONV_WIDTH):
            lag = CONV_WIDTH - 1 - tap
            a = a + up_s[slot, HALO - lag:HALO - lag + FFN_TM, :] * cw_ref[tap:tap + 1, cols]
        gelu = 0.5 * a * (1.0 + lax.erf(a * (2.0 ** -0.5)))
        down = _dot((gelu * val_s[slot]).astype(BF16), wd_ref[cols, :])
        if c == 0:
            acc_s[...] = down
        else:
            acc_s[...] += down

    y = x_ref[...] + acc_s[...]
    if final_norm:
        y = _rms(y, gf_ref[...])
    o_ref[...] = y


def _ffn(x, g, w_up, cw, cb, w_down, gf, seq, final_norm):
    t = x.shape[0]
    halo_blocks = FFN_TM // HALO
    kern = functools.partial(_ffn_kernel, tiles_per_seq=seq // FFN_TM, final_norm=final_norm)
    resident = lambda r, c: pl.BlockSpec((r, c), lambda i: (0, 0), pipeline_mode=pl.Buffered(1))
    return pl.pallas_call(
        kern,
        grid=(t // FFN_TM,),
        in_specs=[
            pl.BlockSpec((FFN_TM, D_MODEL), lambda i: (i, 0)),
            pl.BlockSpec((HALO, D_MODEL), lambda i: (jnp.maximum(i * halo_blocks - 1, 0), 0)),
            resident(1, D_MODEL),
            resident(D_MODEL, 2 * D_FF),
            resident(CONV_WIDTH, D_FF),
            resident(1, D_FF),
            resident(D_FF, D_MODEL),
            resident(1, D_MODEL),
        ],
        out_specs=pl.BlockSpec((FFN_TM, D_MODEL), lambda i: (i, 0)),
        out_shape=jax.ShapeDtypeStruct((t, D_MODEL), F32),
        scratch_shapes=[
            pltpu.VMEM((HALO + FFN_TM, D_MODEL), BF16),
            pltpu.VMEM((2, HALO + FFN_TM, FFN_TF), F32),
            pltpu.VMEM((2, FFN_TM, FFN_TF), F32),
            pltpu.VMEM((FFN_TM, D_MODEL), F32),
        ],
        compiler_params=pltpu.CompilerParams(dimension_semantics=("arbitrary",), vmem_limit_bytes=VMEM_LIMIT),
        name="ffn",
    )(x, x, g.reshape(1, D_MODEL), w_up, cw, cb.reshape(1, D_FF), w_down, gf.reshape(1, D_MODEL))


def kernel(x, norm1, w_in, b_gate, w_br, w_o, norm2, w_up, conv_w, conv_b, w_down, norm_f):
    batch, seq, d = x.shape
    depth = norm1.shape[0]
    assert d == D_MODEL and seq % A_TILE == 0 and seq % FFN_TM == 0
    t = batch * seq
    bias = jnp.asarray(_alibi_bias())
    tri = jnp.asarray(_suffix_sum_matrix(), dtype=BF16)
    xf = x.reshape(t, d)
    for l in range(depth):
        pa, qb, kbd, vbd, pg = _in_proj(xf, norm1[l], w_in[l].astype(BF16))
        oa = _mixer_a(pa, bias, batch, seq)
        ob = _mixer_b(qb, kbd, vbd, tri, batch, seq)
        wbr = w_br[l].astype(BF16)
        x1 = _merge(xf, oa, ob, pg, b_gate[l], wbr[:OUT_A], wbr[OUT_A:], w_o[l].astype(BF16))
        xf = _ffn(x1, norm2[l], w_up[l].astype(BF16), conv_w[l], conv_b[l], w_down[l].astype(BF16),
                  norm_f, seq, final_norm=(l == depth - 1))
    return xf.reshape(batch, seq, d)
```

```python
import functools

import numpy as np
import jax
import jax.numpy as jnp
from jax import lax
from jax.experimental import pallas as pl
from jax.experimental.pallas import tpu as pltpu

D_MODEL = 1024
HEAD_DIM = 64
DSW_GROUPS = ((128, 1), (512, 4), (2048, 16))
N_GROUPS = len(DSW_GROUPS)
HEADS_PER_GROUP = 4
DSW_HEADS = HEADS_PER_GROUP * N_GROUPS
SB_HEADS = 4
W_A = DSW_HEADS * HEAD_DIM
W_B = SB_HEADS * HEAD_DIM
OUT_A = HEADS_PER_GROUP * HEAD_DIM
N_IN = 3 * W_A + 3 * W_B + 2 * D_MODEL
D_FF = 2816
CONV_WIDTH = 3
RMS_EPS = 1e-6
QK_SCALE = HEAD_DIM ** -0.5

LANES = 128
WIN = 128
MASKED = -1e30

VMEM_LIMIT = 56 * 1024 * 1024

F32 = jnp.float32
BF16 = jnp.bfloat16


def _dot(a, b):
    return jnp.dot(a, b, preferred_element_type=F32)


def _dot_nt(a, b):
    return lax.dot_general(a, b, (((1,), (1,)), ((), ())), preferred_element_type=F32)


def _rms(x, g):
    ms = jnp.mean(x * x, axis=-1, keepdims=True)
    return x * lax.rsqrt(ms + RMS_EPS) * g


IN_TM = 512
IN_CHUNK = 256
B_BLK = 128
B_TQ = 256
B_STREAMS = 4
B_CAT = SB_HEADS * B_BLK
LOG2E = 1.4426950408889634
RUN_FLOOR = -160.0


def _in_proj_kernel(x_ref, g_ref, w_ref, pa_ref, qb_ref, kbd_ref, vbd_ref, pg_ref):
    h = _rms(x_ref[...], g_ref[...]).astype(BF16)

    def proj(base):
        return _dot(h, w_ref[:, base:base + IN_CHUNK])

    for c in range(0, 3 * W_A, IN_CHUNK):
        pa_ref[:, c:c + IN_CHUNK] = proj(c)
    base_b = 3 * W_A
    qb_ref[...] = (proj(base_b) * (-QK_SCALE * LOG2E)).astype(BF16)
    lane = lax.broadcasted_iota(jnp.int32, (B_BLK, W_B), 1)
    for ref, base in ((kbd_ref, base_b + W_B), (vbd_ref, base_b + 2 * W_B)):
        val = proj(base)
        for r in range(IN_TM // B_BLK):
            rows = val[r * B_BLK:(r + 1) * B_BLK]
            for hd in range(SB_HEADS):
                keep = (lane >= hd * HEAD_DIM) & (lane < (hd + 1) * HEAD_DIM)
                ref[(r * SB_HEADS + hd) * B_BLK:(r * SB_HEADS + hd + 1) * B_BLK, :] = (
                    jnp.where(keep, rows, 0.0).astype(BF16))
    base_g = 3 * W_A + 3 * W_B
    for c in range(0, 2 * D_MODEL, IN_CHUNK):
        pg_ref[:, c:c + IN_CHUNK] = proj(base_g + c).astype(BF16)


def _in_proj(x, g, w):
    t = x.shape[0]
    return pl.pallas_call(
        _in_proj_kernel,
        grid=(t // IN_TM,),
        in_specs=[
            pl.BlockSpec((IN_TM, D_MODEL), lambda i: (i, 0)),
            pl.BlockSpec((1, D_MODEL), lambda i: (0, 0)),
            pl.BlockSpec((D_MODEL, N_IN), lambda i: (0, 0)),
        ],
        out_specs=[
            pl.BlockSpec((IN_TM, 3 * W_A), lambda i: (i, 0)),
            pl.BlockSpec((IN_TM, W_B), lambda i: (i, 0)),
            pl.BlockSpec((SB_HEADS * IN_TM, W_B), lambda i: (i, 0)),
            pl.BlockSpec((SB_HEADS * IN_TM, W_B), lambda i: (i, 0)),
            pl.BlockSpec((IN_TM, 2 * D_MODEL), lambda i: (i, 0)),
        ],
        out_shape=[
            jax.ShapeDtypeStruct((t, 3 * W_A), F32),
            jax.ShapeDtypeStruct((t, W_B), BF16),
            jax.ShapeDtypeStruct((SB_HEADS * t, W_B), BF16),
            jax.ShapeDtypeStruct((SB_HEADS * t, W_B), BF16),
            jax.ShapeDtypeStruct((t, 2 * D_MODEL), BF16),
        ],
        compiler_params=pltpu.CompilerParams(dimension_semantics=("arbitrary",), vmem_limit_bytes=VMEM_LIMIT),
        name="in_proj",
    )(x, g.reshape(1, D_MODEL), w)


A_TILE = WIN * max(d for _, d in DSW_GROUPS)


def _alibi_bias():
    slopes = 2.0 ** (-8.0 * np.arange(1, DSW_HEADS + 1) / DSW_HEADS)
    qi = np.arange(WIN)[:, None]
    kj = np.arange(2 * WIN)[None, :]
    delta = qi + WIN - kj
    valid = (delta >= 0) & (delta <= WIN)
    out = np.empty((N_GROUPS, 2, 2, 2 * WIN, 2 * WIN), np.float32)
    for g, (_, dil) in enumerate(DSW_GROUPS):
        for h in range(HEADS_PER_GROUP):
            b = -slopes[g * HEADS_PER_GROUP + h] * LOG2E * (delta * dil).astype(np.float32)
            rows = slice((h % 2) * WIN, (h % 2 + 1) * WIN)
            out[g, h // 2, 0, rows] = np.where(valid, b, MASKED)
            out[g, h // 2, 1, rows] = np.where(valid & (kj >= WIN), b, MASKED)
    return out


A_UNROLL = {1: 5, 4: 6, 16: 8}


def _mixer_a_kernel(*refs):
    q_refs = refs[0:3]
    kc_refs = refs[3:6]
    vc_refs = refs[6:9]
    kp_refs = refs[9:12]
    vp_refs = refs[12:15]
    bias_ref = refs[15]
    o_ref = refs[16]
    acc_s, m_s, l_s = refs[17:20]

    first = (pl.program_id(1) == 0).astype(jnp.int32)
    lane = lax.broadcasted_iota(jnp.int32, (WIN, LANES), 1)
    head0 = lane < HEAD_DIM
    ones = jnp.ones((2 * WIN, LANES), BF16)
    n_units = A_TILE // WIN

    for g, (_, dil) in enumerate(DSW_GROUPS):
        q_ref, kc_ref, vc_ref = q_refs[g], kc_refs[g], vc_refs[g]

        def rows(ref, start, dil=dil):
            if dil == 1:
                return ref[pl.ds(start, WIN), :]
            return ref[pl.ds(start, WIN, stride=dil), :]

        def unit(kprev_ref, vprev_ref, prev_off, off, sel, g=g, dil=dil, q_ref=q_ref, kc_ref=kc_ref,
                 vc_ref=vc_ref, rows=rows):
            q = rows(q_ref, off) * (QK_SCALE * LOG2E)
            q2 = jnp.concatenate([jnp.where(head0, q, 0.0), jnp.where(head0, 0.0, q)], axis=0).astype(BF16)
            k = jnp.concatenate([rows(kprev_ref, prev_off), rows(kc_ref, off)], axis=0).astype(BF16)
            v = jnp.concatenate([rows(vprev_ref, prev_off), rows(vc_ref, off)], axis=0).astype(BF16)
            s = _dot_nt(q2, k) + bias_ref[g, sel]
            m = jnp.max(s, axis=-1, keepdims=True)
            p = jnp.exp2(s - m).astype(BF16)
            pvl = _dot(p, jnp.concatenate([v, ones], axis=1))
            dst = pl.ds(off, WIN) if dil == 1 else pl.ds(off, WIN, stride=dil)
            acc_s[g, dst, :] = jnp.where(head0, pvl[0:WIN, 0:LANES], pvl[WIN:2 * WIN, 0:LANES])
            l_s[g, dst, :] = jnp.where(head0, pvl[0:WIN, LANES:2 * LANES], pvl[WIN:2 * WIN, LANES:2 * LANES])
            m_s[g, dst, :] = jnp.where(head0, m[0:WIN], m[WIN:2 * WIN])

        def head_unit(r, unit=unit, kp_ref=kp_refs[g], vp_ref=vp_refs[g]):
            unit(kp_ref, vp_ref, r, r, first)

        def tail_unit(idx, unit=unit, kc_ref=kc_ref, vc_ref=vc_ref, dil=dil):
            shift = dil.bit_length() - 1
            u = idx >> shift if isinstance(idx, int) else lax.shift_right_logical(idx, shift)
            off = u * (WIN * dil) + (idx & (dil - 1))
            unit(kc_ref, vc_ref, off - WIN * dil, off, 0)

        unroll = A_UNROLL[dil]
        for fn, lo, hi in ((head_unit, 0, dil), (tail_unit, dil, n_units)):
            count = hi - lo
            if count == 0:
                continue
            if count <= unroll:
                for idx in range(lo, hi):
                    fn(idx)
                continue
            assert count % unroll == 0

            def body(t, c, fn=fn, lo=lo, unroll=unroll):
                for uu in range(unroll):
                    fn(lo + t * unroll + uu)
                return c

            lax.fori_loop(0, count // unroll, body, 0)

    m_all = jnp.maximum(jnp.maximum(m_s[0], m_s[1]), m_s[2])
    num = jnp.zeros((A_TILE, LANES), F32)
    den = jnp.zeros((A_TILE, LANES), F32)
    for g in range(N_GROUPS):
        w = jnp.exp2(m_s[g] - m_all)
        num = num + w * acc_s[g]
        den = den + w * l_s[g]
    o_ref[...] = (num / den).astype(o_ref.dtype)


def _mixer_a(pa, bias, batch, seq):
    t = pa.shape[0]
    tiles = seq // A_TILE
    qcol = lambda g: (lambda b, n, hp, g=g: (b * tiles + n, 0 * 6 + 2 * g + hp))
    kcol = lambda g: (lambda b, n, hp, g=g: (b * tiles + n, 1 * 6 + 2 * g + hp))
    vcol = lambda g: (lambda b, n, hp, g=g: (b * tiles + n, 2 * 6 + 2 * g + hp))

    def prev(kind, g):
        per_tile = A_TILE // (WIN * DSW_GROUPS[g][1])
        return lambda b, n, hp: (jnp.maximum((b * tiles + n) * per_tile - 1, 0), kind * 6 + 2 * g + hp)

    in_specs = (
        [pl.BlockSpec((A_TILE, LANES), qcol(g)) for g in range(N_GROUPS)]
        + [pl.BlockSpec((A_TILE, LANES), kcol(g)) for g in range(N_GROUPS)]
        + [pl.BlockSpec((A_TILE, LANES), vcol(g)) for g in range(N_GROUPS)]
        + [pl.BlockSpec((WIN * DSW_GROUPS[g][1], LANES), prev(1, g)) for g in range(N_GROUPS)]
        + [pl.BlockSpec((WIN * DSW_GROUPS[g][1], LANES), prev(2, g)) for g in range(N_GROUPS)]
        + [pl.BlockSpec((N_GROUPS, None, 2, 2 * WIN, 2 * WIN), lambda b, n, hp: (0, hp, 0, 0, 0))]
    )
    return pl.pallas_call(
        _mixer_a_kernel,
        grid=(batch, tiles, 2),
        in_specs=in_specs,
        out_specs=pl.BlockSpec((A_TILE, LANES), lambda b, n, hp: (b * tiles + n, hp)),
        out_shape=jax.ShapeDtypeStruct((t, OUT_A), BF16),
        scratch_shapes=[pltpu.VMEM((N_GROUPS, A_TILE, LANES), F32)] * 3,
        compiler_params=pltpu.CompilerParams(
            dimension_semantics=("arbitrary", "arbitrary", "arbitrary"), vmem_limit_bytes=VMEM_LIMIT),
        name="mixer_a",
    )(*([pa] * 15), bias)


def _suffix_sum_matrix():
    j = np.arange(B_BLK)[:, None]
    s = np.arange(B_BLK)[None, :]
    later = (j > s).astype(np.float32)
    ones = np.ones((B_BLK, B_BLK), np.float32)
    zero = np.zeros((B_BLK, B_BLK), np.float32)
    return np.block([[later, zero, ones, zero], [zero, later, zero, ones]])


def _mixer_b_kernel(q_ref, kbd_ref, vbd_ref, tri_ref, o_ref, acc_s, run_s):
    step = pl.program_id(1)
    tri = tri_ref[...]
    t_loc = lax.broadcasted_iota(jnp.int32, (B_TQ, B_CAT), 0)
    s_loc = jnp.bitwise_and(lax.broadcasted_iota(jnp.int32, (B_TQ, B_CAT), 1), B_BLK - 1)
    per_q = B_TQ // B_BLK

    def block(s, j, diag_offset, first):
        start = pl.multiple_of(j * B_CAT, B_CAT)
        zn = _dot_nt(q_ref[s * B_TQ:(s + 1) * B_TQ, :], kbd_ref[pl.ds(start, B_CAT), :])
        soft = jnp.log2(1.0 + jnp.exp2(-jnp.abs(zn)))
        log_stay = jnp.minimum(zn, 0.0) - soft
        if diag_offset is not None:
            causal = s_loc + diag_offset < t_loc
            log_stay = jnp.where(causal, log_stay, 0.0)
        ls16 = log_stay.astype(BF16)
        half = SB_HEADS // 2 * B_BLK
        sums = [_dot(ls16[:, p * half:(p + 1) * half], tri) for p in range(2)]
        later = jnp.concatenate([x[:, 0:half] for x in sums], axis=1)
        whole = jnp.concatenate([x[:, half:2 * half] for x in sums], axis=1)
        expo = (log_stay - zn) + later
        if not first:
            run = run_s[s]
            expo = expo + run
            whole = whole + run
        p = jnp.exp2(expo)
        if diag_offset is not None:
            p = jnp.where(causal, p, 0.0)
        run_s[s] = whole
        pv = _dot(p.astype(BF16), vbd_ref[pl.ds(start, B_CAT), :])
        if first:
            acc_s[s] = pv
        else:
            acc_s[s] += pv

    def diagonal(s):
        i = step * B_STREAMS + s
        for d in range(per_q - 1, -1, -1):
            block(s, i * per_q + d, d * B_BLK, first=(d == per_q - 1))

    def trip(s, jj):
        i = step * B_STREAMS + s
        for d in range(per_q):
            block(s, (i - jj) * per_q - 1 - d, None, first=False)

    @pl.when(step > 0)
    def _():
        for s in range(B_STREAMS):
            diagonal(s)
            trip(s, 0)

    @pl.when(step == 0)
    def _():
        diagonal(0)
        for s in range(1, B_STREAMS):
            diagonal(s)
            trip(s, 0)

    def alive(s):
        run = run_s[s]
        top = run[:, 0:B_BLK]
        for h in range(1, SB_HEADS):
            top = jnp.maximum(top, run[:, h * B_BLK:(h + 1) * B_BLK])
        return jnp.max(top) > RUN_FLOOR

    for s in range(B_STREAMS):
        i = step * B_STREAMS + s

        def cond(c, i=i):
            jj, live = c
            return jnp.logical_and(jj < i, live)

        def body(c, s=s):
            jj, _ = c
            trip(s, jj)
            return jj + 1, alive(s)

        lax.while_loop(cond, body, (jnp.int32(1), alive(s)))

    o_ref[...] = acc_s[...].reshape(B_STREAMS * B_TQ, W_B).astype(o_ref.dtype)


def _mixer_b(qb, kbd, vbd, tri, batch, seq):
    t = qb.shape[0]
    rows = B_STREAMS * B_TQ
    whole_seq = pl.BlockSpec((None, SB_HEADS * seq, W_B), lambda b, i: (b, 0, 0), pipeline_mode=pl.Buffered(1))
    out = pl.pallas_call(
        _mixer_b_kernel,
        grid=(batch, seq // rows),
        in_specs=[
            pl.BlockSpec((None, rows, W_B), lambda b, i: (b, i, 0)),
            whole_seq,
            whole_seq,
            pl.BlockSpec((2 * B_BLK, 4 * B_BLK), lambda b, i: (0, 0)),
        ],
        out_specs=pl.BlockSpec((None, rows, W_B), lambda b, i: (b, i, 0)),
        out_shape=jax.ShapeDtypeStruct((batch, seq, W_B), BF16),
        scratch_shapes=[pltpu.VMEM((B_STREAMS, B_TQ, W_B), F32), pltpu.VMEM((B_STREAMS, B_TQ, B_CAT), F32)],
        compiler_params=pltpu.CompilerParams(
            dimension_semantics=("arbitrary", "arbitrary"), vmem_limit_bytes=VMEM_LIMIT),
        name="mixer_b",
    )(qb.reshape(batch, seq, W_B), kbd.reshape(batch, SB_HEADS * seq, W_B),
      vbd.reshape(batch, SB_HEADS * seq, W_B), tri)
    return out.reshape(t, W_B)


FFN_TM = 512
FFN_TF = 256
HALO = 16


def _merge_ffn_kernel(x_ref, oa_ref, ob_ref, pg_ref, bg_ref, wbr_ref, wo_ref, g_ref, wup_ref, cw_ref, cb_ref,
                      wd_ref, gf_ref, o_ref, x1_s, h_s, up_s, val_s, acc_s, *, tiles_per_seq, final_norm):
    i = pl.program_id(0)

    gates = jax.nn.sigmoid(pg_ref[...].astype(F32) + bg_ref[...])
    ya = _dot(oa_ref[...], wbr_ref[0:OUT_A, :])
    yb = _dot(ob_ref[...], wbr_ref[OUT_A:OUT_A + W_B, :])
    merged = gates[:, 0:D_MODEL] * ya + gates[:, D_MODEL:2 * D_MODEL] * yb
    x1 = x_ref[...] + _dot(merged.astype(BF16), wo_ref[...])
    x1_s[...] = x1

    @pl.when(i == 0)
    def _():
        h_s[FFN_TM:FFN_TM + HALO, :] = jnp.zeros((HALO, D_MODEL), BF16)

    h_s[0:HALO, :] = h_s[FFN_TM:FFN_TM + HALO, :]
    h_s[HALO:HALO + FFN_TM, :] = _rms(x1, g_ref[...]).astype(BF16)
    seq_start = (i % tiles_per_seq) == 0
    n_chunks = D_FF // FFN_TF

    def project(c):
        slot = c % 2
        up = _dot(h_s[...], wup_ref[:, c * FFN_TF:(c + 1) * FFN_TF])
        up_s[slot, 0:HALO, :] = jnp.where(seq_start, 0.0, up[0:HALO])
        up_s[slot, HALO:HALO + FFN_TM, :] = up[HALO:HALO + FFN_TM]
        val_s[slot] = _dot(h_s[HALO:HALO + FFN_TM, :], wup_ref[:, D_FF + c * FFN_TF:D_FF + (c + 1) * FFN_TF])

    project(0)
    for c in range(n_chunks):
        if c + 1 < n_chunks:
            project(c + 1)
        slot = c % 2
        cols = slice(c * FFN_TF, (c + 1) * FFN_TF)
        a = cb_ref[:, cols]
        for tap in range(CONV_WIDTH):
            lag = CONV_WIDTH - 1 - tap
            a = a + up_s[slot, HALO - lag:HALO - lag + FFN_TM, :] * cw_ref[tap:tap + 1, cols]
        gelu = 0.5 * a * (1.0 + lax.erf(a * (2.0 ** -0.5)))
        down = _dot((gelu * val_s[slot]).astype(BF16), wd_ref[cols, :])
        if c == 0:
            acc_s[...] = down
        else:
            acc_s[...] += down

    y = x1_s[...] + acc_s[...]
    if final_norm:
        y = _rms(y, gf_ref[...])
    o_ref[...] = y


def _merge_ffn(x, oa, ob, pg, bg, wbr, wo, g, w_up, cw, cb, w_down, gf, seq, final_norm):
    t = x.shape[0]
    kern = functools.partial(_merge_ffn_kernel, tiles_per_seq=seq // FFN_TM, final_norm=final_norm)
    row = lambda w: pl.BlockSpec((FFN_TM, w), lambda i: (i, 0))
    resident = lambda r, c: pl.BlockSpec((r, c), lambda i: (0, 0), pipeline_mode=pl.Buffered(1))
    return pl.pallas_call(
        kern,
        grid=(t // FFN_TM,),
        in_specs=[
            row(D_MODEL), row(OUT_A), row(W_B), row(2 * D_MODEL),
            resident(1, 2 * D_MODEL),
            resident(OUT_A + W_B, D_MODEL),
            resident(D_MODEL, D_MODEL),
            resident(1, D_MODEL),
            resident(D_MODEL, 2 * D_FF),
            resident(CONV_WIDTH, D_FF),
            resident(1, D_FF),
            resident(D_FF, D_MODEL),
            resident(1, D_MODEL),
        ],
        out_specs=row(D_MODEL),
        out_shape=jax.ShapeDtypeStruct((t, D_MODEL), F32),
        scratch_shapes=[
            pltpu.VMEM((FFN_TM, D_MODEL), F32),
            pltpu.VMEM((HALO + FFN_TM, D_MODEL), BF16),
            pltpu.VMEM((2, HALO + FFN_TM, FFN_TF), F32),
            pltpu.VMEM((2, FFN_TM, FFN_TF), F32),
            pltpu.VMEM((FFN_TM, D_MODEL), F32),
        ],
        compiler_params=pltpu.CompilerParams(dimension_semantics=("arbitrary",), vmem_limit_bytes=VMEM_LIMIT),
        name="merge_ffn",
    )(x, oa, ob, pg, bg.reshape(1, 2 * D_MODEL), wbr, wo, g.reshape(1, D_MODEL), w_up, cw,
      cb.reshape(1, D_FF), w_down, gf.reshape(1, D_MODEL))


def kernel(x, norm1, w_in, b_gate, w_br, w_o, norm2, w_up, conv_w, conv_b, w_down, norm_f):
    batch, seq, d = x.shape
    depth = norm1.shape[0]
    assert d == D_MODEL and seq % A_TILE == 0 and seq % FFN_TM == 0
    t = batch * seq
    bias = jnp.asarray(_alibi_bias())
    tri = jnp.asarray(_suffix_sum_matrix(), dtype=BF16)
    xf = x.reshape(t, d)
    for l in range(depth):
        pa, qb, kbd, vbd, pg = _in_proj(xf, norm1[l], w_in[l].astype(BF16))
        oa = _mixer_a(pa, bias, batch, seq)
        ob = _mixer_b(qb, kbd, vbd, tri, batch, seq)
        xf = _merge_ffn(xf, oa, ob, pg, b_gate[l], w_br[l].astype(BF16), w_o[l].astype(BF16), norm2[l],
                        w_up[l].astype(BF16), conv_w[l], conv_b[l], w_down[l].astype(BF16), norm_f, seq,
                        final_norm=(l == depth - 1))
    return xf.reshape(batch, seq, d)
```

```python
import functools

import numpy as np
import jax
import jax.numpy as jnp
from jax import lax
from jax.experimental import pallas as pl
from jax.experimental.pallas import tpu as pltpu

D_MODEL = 1024
HEAD_DIM = 64
DSW_GROUPS = ((128, 1), (512, 4), (2048, 16))
N_GROUPS = len(DSW_GROUPS)
HEADS_PER_GROUP = 4
DSW_HEADS = HEADS_PER_GROUP * N_GROUPS
SB_HEADS = 4
W_A = DSW_HEADS * HEAD_DIM
W_B = SB_HEADS * HEAD_DIM
OUT_A = HEADS_PER_GROUP * HEAD_DIM
N_IN = 3 * W_A + 3 * W_B + 2 * D_MODEL
D_FF = 2816
CONV_WIDTH = 3
RMS_EPS = 1e-6
QK_SCALE = HEAD_DIM ** -0.5

LANES = 128
WIN = 128
MASKED = -1e30

VMEM_LIMIT = 56 * 1024 * 1024

F32 = jnp.float32
BF16 = jnp.bfloat16


def _dot(a, b):
    return jnp.dot(a, b, preferred_element_type=F32)


def _dot_nt(a, b):
    return lax.dot_general(a, b, (((1,), (1,)), ((), ())), preferred_element_type=F32)


def _rms(x, g):
    ms = jnp.mean(x * x, axis=-1, keepdims=True)
    return x * lax.rsqrt(ms + RMS_EPS) * g


IN_TM = 512
IN_CHUNK = 256
B_BLK = 128
B_TQ = 256
B_STREAMS = 4
B_CAT = SB_HEADS * B_BLK
LOG2E = 1.4426950408889634
RUN_FLOOR = -160.0


def _in_proj_kernel(x_ref, g_ref, w_ref, pa_ref, qb_ref, kbd_ref, vbd_ref, pg_ref):
    h = _rms(x_ref[...], g_ref[...]).astype(BF16)

    def proj(base):
        return _dot(h, w_ref[:, base:base + IN_CHUNK])

    for c in range(0, 3 * W_A, IN_CHUNK):
        pa_ref[:, c:c + IN_CHUNK] = proj(c)
    base_b = 3 * W_A
    qb_ref[...] = (proj(base_b) * (-QK_SCALE * LOG2E)).astype(BF16)
    lane = lax.broadcasted_iota(jnp.int32, (B_BLK, W_B), 1)
    for ref, base in ((kbd_ref, base_b + W_B), (vbd_ref, base_b + 2 * W_B)):
        val = proj(base)
        for r in range(IN_TM // B_BLK):
            rows = val[r * B_BLK:(r + 1) * B_BLK]
            for hd in range(SB_HEADS):
                keep = (lane >= hd * HEAD_DIM) & (lane < (hd + 1) * HEAD_DIM)
                ref[(r * SB_HEADS + hd) * B_BLK:(r * SB_HEADS + hd + 1) * B_BLK, :] = (
                    jnp.where(keep, rows, 0.0).astype(BF16))
    base_g = 3 * W_A + 3 * W_B
    for c in range(0, 2 * D_MODEL, IN_CHUNK):
        pg_ref[:, c:c + IN_CHUNK] = proj(base_g + c).astype(BF16)


def _layer_spec(layer, rows, cols, **kw):
    return pl.BlockSpec((None, rows, cols), lambda i: (layer, 0, 0), **kw)


def _in_proj(x, g, w, layer):
    t = x.shape[0]
    return pl.pallas_call(
        _in_proj_kernel,
        grid=(t // IN_TM,),
        in_specs=[
            pl.BlockSpec((IN_TM, D_MODEL), lambda i: (i, 0)),
            _layer_spec(layer, 1, D_MODEL),
            _layer_spec(layer, D_MODEL, N_IN),
        ],
        out_specs=[
            pl.BlockSpec((IN_TM, 3 * W_A), lambda i: (i, 0)),
            pl.BlockSpec((IN_TM, W_B), lambda i: (i, 0)),
            pl.BlockSpec((SB_HEADS * IN_TM, W_B), lambda i: (i, 0)),
            pl.BlockSpec((SB_HEADS * IN_TM, W_B), lambda i: (i, 0)),
            pl.BlockSpec((IN_TM, 2 * D_MODEL), lambda i: (i, 0)),
        ],
        out_shape=[
            jax.ShapeDtypeStruct((t, 3 * W_A), F32),
            jax.ShapeDtypeStruct((t, W_B), BF16),
            jax.ShapeDtypeStruct((SB_HEADS * t, W_B), BF16),
            jax.ShapeDtypeStruct((SB_HEADS * t, W_B), BF16),
            jax.ShapeDtypeStruct((t, 2 * D_MODEL), BF16),
        ],
        compiler_params=pltpu.CompilerParams(dimension_semantics=("arbitrary",), vmem_limit_bytes=VMEM_LIMIT),
        name="in_proj",
    )(x, g.reshape(-1, 1, D_MODEL), w)


A_TILE = WIN * max(d for _, d in DSW_GROUPS)


def _alibi_bias():
    slopes = 2.0 ** (-8.0 * np.arange(1, DSW_HEADS + 1) / DSW_HEADS)
    qi = np.arange(WIN)[:, None]
    kj = np.arange(2 * WIN)[None, :]
    delta = qi + WIN - kj
    valid = (delta >= 0) & (delta <= WIN)
    out = np.empty((N_GROUPS, 2, 2, 2 * WIN, 2 * WIN), np.float32)
    for g, (_, dil) in enumerate(DSW_GROUPS):
        for h in range(HEADS_PER_GROUP):
            b = -slopes[g * HEADS_PER_GROUP + h] * LOG2E * (delta * dil).astype(np.float32)
            rows = slice((h % 2) * WIN, (h % 2 + 1) * WIN)
            out[g, h // 2, 0, rows] = np.where(valid, b, MASKED)
            out[g, h // 2, 1, rows] = np.where(valid & (kj >= WIN), b, MASKED)
    return out


A_UNROLL = {1: 5, 4: 6, 16: 8}


def _mixer_a_kernel(*refs):
    q_refs = refs[0:3]
    kc_refs = refs[3:6]
    vc_refs = refs[6:9]
    kp_refs = refs[9:12]
    vp_refs = refs[12:15]
    bias_ref = refs[15]
    o_ref = refs[16]
    acc_s, m_s, l_s = refs[17:20]

    first = (pl.program_id(1) == 0).astype(jnp.int32)
    lane = lax.broadcasted_iota(jnp.int32, (WIN, LANES), 1)
    head0 = lane < HEAD_DIM
    ones = jnp.ones((2 * WIN, LANES), BF16)
    n_units = A_TILE // WIN

    for g, (_, dil) in enumerate(DSW_GROUPS):
        q_ref, kc_ref, vc_ref = q_refs[g], kc_refs[g], vc_refs[g]

        def rows(ref, start, dil=dil):
            if dil == 1:
                return ref[pl.ds(start, WIN), :]
            return ref[pl.ds(start, WIN, stride=dil), :]

        def unit(kprev_ref, vprev_ref, prev_off, off, sel, g=g, dil=dil, q_ref=q_ref, kc_ref=kc_ref,
                 vc_ref=vc_ref, rows=rows):
            q = rows(q_ref, off) * (QK_SCALE * LOG2E)
            q2 = jnp.concatenate([jnp.where(head0, q, 0.0), jnp.where(head0, 0.0, q)], axis=0).astype(BF16)
            k = jnp.concatenate([rows(kprev_ref, prev_off), rows(kc_ref, off)], axis=0).astype(BF16)
            v = jnp.concatenate([rows(vprev_ref, prev_off), rows(vc_ref, off)], axis=0).astype(BF16)
            s = _dot_nt(q2, k) + bias_ref[g, sel]
            m = jnp.max(s, axis=-1, keepdims=True)
            p = jnp.exp2(s - m).astype(BF16)
            pvl = _dot(p, jnp.concatenate([v, ones], axis=1))
            dst = pl.ds(off, WIN) if dil == 1 else pl.ds(off, WIN, stride=dil)
            acc_s[g, dst, :] = jnp.where(head0, pvl[0:WIN, 0:LANES], pvl[WIN:2 * WIN, 0:LANES])
            l_s[g, dst, :] = jnp.where(head0, pvl[0:WIN, LANES:2 * LANES], pvl[WIN:2 * WIN, LANES:2 * LANES])
            m_s[g, dst, :] = jnp.where(head0, m[0:WIN], m[WIN:2 * WIN])

        def head_unit(r, unit=unit, kp_ref=kp_refs[g], vp_ref=vp_refs[g]):
            unit(kp_ref, vp_ref, r, r, first)

        def tail_unit(idx, unit=unit, kc_ref=kc_ref, vc_ref=vc_ref, dil=dil):
            shift = dil.bit_length() - 1
            u = idx >> shift if isinstance(idx, int) else lax.shift_right_logical(idx, shift)
            off = u * (WIN * dil) + (idx & (dil - 1))
            unit(kc_ref, vc_ref, off - WIN * dil, off, 0)

        unroll = A_UNROLL[dil]
        for fn, lo, hi in ((head_unit, 0, dil), (tail_unit, dil, n_units)):
            count = hi - lo
            if count == 0:
                continue
            if count <= unroll:
                for idx in range(lo, hi):
                    fn(idx)
                continue
            assert count % unroll == 0

            def body(t, c, fn=fn, lo=lo, unroll=unroll):
                for uu in range(unroll):
                    fn(lo + t * unroll + uu)
                return c

            lax.fori_loop(0, count // unroll, body, 0)

    m_all = jnp.maximum(jnp.maximum(m_s[0], m_s[1]), m_s[2])
    num = jnp.zeros((A_TILE, LANES), F32)
    den = jnp.zeros((A_TILE, LANES), F32)
    for g in range(N_GROUPS):
        w = jnp.exp2(m_s[g] - m_all)
        num = num + w * acc_s[g]
        den = den + w * l_s[g]
    o_ref[...] = (num / den).astype(o_ref.dtype)


def _mixer_a(pa, bias, batch, seq):
    t = pa.shape[0]
    tiles = seq // A_TILE
    qcol = lambda g: (lambda b, n, hp, g=g: (b * tiles + n, 0 * 6 + 2 * g + hp))
    kcol = lambda g: (lambda b, n, hp, g=g: (b * tiles + n, 1 * 6 + 2 * g + hp))
    vcol = lambda g: (lambda b, n, hp, g=g: (b * tiles + n, 2 * 6 + 2 * g + hp))

    def prev(kind, g):
        per_tile = A_TILE // (WIN * DSW_GROUPS[g][1])
        return lambda b, n, hp: (jnp.maximum((b * tiles + n) * per_tile - 1, 0), kind * 6 + 2 * g + hp)

    in_specs = (
        [pl.BlockSpec((A_TILE, LANES), qcol(g)) for g in range(N_GROUPS)]
        + [pl.BlockSpec((A_TILE, LANES), kcol(g)) for g in range(N_GROUPS)]
        + [pl.BlockSpec((A_TILE, LANES), vcol(g)) for g in range(N_GROUPS)]
        + [pl.BlockSpec((WIN * DSW_GROUPS[g][1], LANES), prev(1, g)) for g in range(N_GROUPS)]
        + [pl.BlockSpec((WIN * DSW_GROUPS[g][1], LANES), prev(2, g)) for g in range(N_GROUPS)]
        + [pl.BlockSpec((N_GROUPS, None, 2, 2 * WIN, 2 * WIN), lambda b, n, hp: (0, hp, 0, 0, 0))]
    )
    return pl.pallas_call(
        _mixer_a_kernel,
        grid=(batch, tiles, 2),
        in_specs=in_specs,
        out_specs=pl.BlockSpec((A_TILE, LANES), lambda b, n, hp: (b * tiles + n, hp)),
        out_shape=jax.ShapeDtypeStruct((t, OUT_A), BF16),
        scratch_shapes=[pltpu.VMEM((N_GROUPS, A_TILE, LANES), F32)] * 3,
        compiler_params=pltpu.CompilerParams(
            dimension_semantics=("arbitrary", "arbitrary", "arbitrary"), vmem_limit_bytes=VMEM_LIMIT),
        name="mixer_a",
    )(*([pa] * 15), bias)


def _suffix_sum_matrix():
    j = np.arange(B_BLK)[:, None]
    s = np.arange(B_BLK)[None, :]
    later = (j > s).astype(np.float32)
    ones = np.ones((B_BLK, B_BLK), np.float32)
    zero = np.zeros((B_BLK, B_BLK), np.float32)
    return np.block([[later, zero, ones, zero], [zero, later, zero, ones]])


def _mixer_b_kernel(q_ref, kbd_ref, vbd_ref, tri_ref, o_ref, acc_s, run_s):
    step = pl.program_id(1)
    tri = tri_ref[...]
    t_loc = lax.broadcasted_iota(jnp.int32, (B_TQ, B_CAT), 0)
    s_loc = jnp.bitwise_and(lax.broadcasted_iota(jnp.int32, (B_TQ, B_CAT), 1), B_BLK - 1)
    per_q = B_TQ // B_BLK

    def block(s, j, diag_offset, first):
        start = pl.multiple_of(j * B_CAT, B_CAT)
        zn = _dot_nt(q_ref[s * B_TQ:(s + 1) * B_TQ, :], kbd_ref[pl.ds(start, B_CAT), :])
        soft = jnp.log2(1.0 + jnp.exp2(-jnp.abs(zn)))
        log_stay = jnp.minimum(zn, 0.0) - soft
        if diag_offset is not None:
            causal = s_loc + diag_offset < t_loc
            log_stay = jnp.where(causal, log_stay, 0.0)
        ls16 = log_stay.astype(BF16)
        half = SB_HEADS // 2 * B_BLK
        sums = [_dot(ls16[:, p * half:(p + 1) * half], tri) for p in range(2)]
        later = jnp.concatenate([x[:, 0:half] for x in sums], axis=1)
        whole = jnp.concatenate([x[:, half:2 * half] for x in sums], axis=1)
        expo = (log_stay - zn) + later
        if not first:
            run = run_s[s]
            expo = expo + run
            whole = whole + run
        p = jnp.exp2(expo)
        if diag_offset is not None:
            p = jnp.where(causal, p, 0.0)
        run_s[s] = whole
        pv = _dot(p.astype(BF16), vbd_ref[pl.ds(start, B_CAT), :])
        if first:
            acc_s[s] = pv
        else:
            acc_s[s] += pv

    def diagonal(s):
        i = step * B_STREAMS + s
        for d in range(per_q - 1, -1, -1):
            block(s, i * per_q + d, d * B_BLK, first=(d == per_q - 1))

    def trip(s, jj):
        i = step * B_STREAMS + s
        for d in range(per_q):
            block(s, (i - jj) * per_q - 1 - d, None, first=False)

    @pl.when(step > 0)
    def _():
        for s in range(B_STREAMS):
            diagonal(s)
            trip(s, 0)

    @pl.when(step == 0)
    def _():
        diagonal(0)
        for s in range(1, B_STREAMS):
            diagonal(s)
            trip(s, 0)

    def alive(s):
        run = run_s[s]
        top = run[:, 0:B_BLK]
        for h in range(1, SB_HEADS):
            top = jnp.maximum(top, run[:, h * B_BLK:(h + 1) * B_BLK])
        return jnp.max(top) > RUN_FLOOR

    for s in range(B_STREAMS):
        i = step * B_STREAMS + s

        def cond(c, i=i):
            jj, live = c
            return jnp.logical_and(jj < i, live)

        def body(c, s=s):
            jj, _ = c
            trip(s, jj)
            return jj + 1, alive(s)

        lax.while_loop(cond, body, (jnp.int32(1), alive(s)))

    o_ref[...] = acc_s[...].reshape(B_STREAMS * B_TQ, W_B).astype(o_ref.dtype)


def _mixer_b(qb, kbd, vbd, tri, batch, seq):
    t = qb.shape[0]
    rows = B_STREAMS * B_TQ
    whole_seq = pl.BlockSpec((None, SB_HEADS * seq, W_B), lambda b, i: (b, 0, 0), pipeline_mode=pl.Buffered(1))
    out = pl.pallas_call(
        _mixer_b_kernel,
        grid=(batch, seq // rows),
        in_specs=[
            pl.BlockSpec((None, rows, W_B), lambda b, i: (b, i, 0)),
            whole_seq,
            whole_seq,
            pl.BlockSpec((2 * B_BLK, 4 * B_BLK), lambda b, i: (0, 0)),
        ],
        out_specs=pl.BlockSpec((None, rows, W_B), lambda b, i: (b, i, 0)),
        out_shape=jax.ShapeDtypeStruct((batch, seq, W_B), BF16),
        scratch_shapes=[pltpu.VMEM((B_STREAMS, B_TQ, W_B), F32), pltpu.VMEM((B_STREAMS, B_TQ, B_CAT), F32)],
        compiler_params=pltpu.CompilerParams(
            dimension_semantics=("arbitrary", "arbitrary"), vmem_limit_bytes=VMEM_LIMIT),
        name="mixer_b",
    )(qb.reshape(batch, seq, W_B), kbd.reshape(batch, SB_HEADS * seq, W_B),
      vbd.reshape(batch, SB_HEADS * seq, W_B), tri)
    return out.reshape(t, W_B)


FFN_TM = 512
FFN_TF = 256
HALO = 16


def _merge_ffn_kernel(x_ref, oa_ref, ob_ref, pg_ref, bg_ref, wbr_ref, wo_ref, g_ref, wup_ref, cw_ref, cb_ref,
                      wd_ref, gf_ref, o_ref, x1_s, h_s, up_s, val_s, acc_s, *, tiles_per_seq, final_norm):
    i = pl.program_id(0)

    gates = jax.nn.sigmoid(pg_ref[...].astype(F32) + bg_ref[...])
    ya = _dot(oa_ref[...], wbr_ref[0:OUT_A, :])
    yb = _dot(ob_ref[...], wbr_ref[OUT_A:OUT_A + W_B, :])
    merged = gates[:, 0:D_MODEL] * ya + gates[:, D_MODEL:2 * D_MODEL] * yb
    x1 = x_ref[...] + _dot(merged.astype(BF16), wo_ref[...])
    x1_s[...] = x1

    @pl.when(i == 0)
    def _():
        h_s[FFN_TM:FFN_TM + HALO, :] = jnp.zeros((HALO, D_MODEL), BF16)

    h_s[0:HALO, :] = h_s[FFN_TM:FFN_TM + HALO, :]
    h_s[HALO:HALO + FFN_TM, :] = _rms(x1, g_ref[...]).astype(BF16)
    seq_start = (i % tiles_per_seq) == 0
    n_chunks = D_FF // FFN_TF

    def project(c):
        slot = c % 2
        up = _dot(h_s[...], wup_ref[:, c * FFN_TF:(c + 1) * FFN_TF])
        for ls in range(FFN_TF // LANES):
            lanes = slice(ls * LANES, (ls + 1) * LANES)
            up_s[slot, ls, pl.ds(0, HALO, stride=2), :] = jnp.where(seq_start, 0.0, up[0:HALO, lanes])
            up_s[slot, ls, pl.ds(2 * HALO, FFN_TM, stride=2), :] = up[HALO:HALO + FFN_TM, lanes]
        val_s[slot] = _dot(h_s[HALO:HALO + FFN_TM, :], wup_ref[:, D_FF + c * FFN_TF:D_FF + (c + 1) * FFN_TF])

    project(0)
    for c in range(n_chunks):
        if c + 1 < n_chunks:
            project(c + 1)
        slot = c % 2
        cols = slice(c * FFN_TF, (c + 1) * FFN_TF)
        parts = []
        for ls in range(FFN_TF // LANES):
            lanes = slice(c * FFN_TF + ls * LANES, c * FFN_TF + (ls + 1) * LANES)
            a = cb_ref[:, lanes]
            for tap in range(CONV_WIDTH):
                lag = CONV_WIDTH - 1 - tap
                a = a + up_s[slot, ls, pl.ds(2 * (HALO - lag), FFN_TM, stride=2), :] * cw_ref[tap:tap + 1, lanes]
            parts.append(a)
        a = jnp.concatenate(parts, axis=1)
        gelu = 0.5 * a * (1.0 + lax.erf(a * (2.0 ** -0.5)))
        down = _dot((gelu * val_s[slot]).astype(BF16), wd_ref[cols, :])
        if c == 0:
            acc_s[...] = down
        else:
            acc_s[...] += down

    y = x1_s[...] + acc_s[...]
    if final_norm:
        y = _rms(y, gf_ref[...])
    o_ref[...] = y


def _merge_ffn(x, oa, ob, pg, bg, wbr, wo, g, w_up, cw, cb, w_down, gf, layer, seq, final_norm):
    t = x.shape[0]
    kern = functools.partial(_merge_ffn_kernel, tiles_per_seq=seq // FFN_TM, final_norm=final_norm)
    row = lambda w: pl.BlockSpec((FFN_TM, w), lambda i: (i, 0))
    resident = lambda r, c: _layer_spec(layer, r, c, pipeline_mode=pl.Buffered(1))
    return pl.pallas_call(
        kern,
        grid=(t // FFN_TM,),
        in_specs=[
            row(D_MODEL), row(OUT_A), row(W_B), row(2 * D_MODEL),
            resident(1, 2 * D_MODEL),
            resident(OUT_A + W_B, D_MODEL),
            resident(D_MODEL, D_MODEL),
            resident(1, D_MODEL),
            resident(D_MODEL, 2 * D_FF),
            resident(CONV_WIDTH, D_FF),
            resident(1, D_FF),
            resident(D_FF, D_MODEL),
            pl.BlockSpec((1, D_MODEL), lambda i: (0, 0)),
        ],
        out_specs=row(D_MODEL),
        out_shape=jax.ShapeDtypeStruct((t, D_MODEL), F32),
        scratch_shapes=[
            pltpu.VMEM((FFN_TM, D_MODEL), F32),
            pltpu.VMEM((HALO + FFN_TM, D_MODEL), BF16),
            pltpu.VMEM((2, FFN_TF // LANES, 2 * (HALO + FFN_TM), LANES), F32),
            pltpu.VMEM((2, FFN_TM, FFN_TF), F32),
            pltpu.VMEM((FFN_TM, D_MODEL), F32),
        ],
        compiler_params=pltpu.CompilerParams(dimension_semantics=("arbitrary",), vmem_limit_bytes=VMEM_LIMIT),
        name="merge_ffn",
    )(x, oa, ob, pg, bg.reshape(-1, 1, 2 * D_MODEL), wbr, wo, g.reshape(-1, 1, D_MODEL), w_up, cw,
      cb.reshape(-1, 1, D_FF), w_down, gf.reshape(1, D_MODEL))


def kernel(x, norm1, w_in, b_gate, w_br, w_o, norm2, w_up, conv_w, conv_b, w_down, norm_f):
    batch, seq, d = x.shape
    depth = norm1.shape[0]
    assert d == D_MODEL and seq % A_TILE == 0 and seq % FFN_TM == 0
    t = batch * seq
    bias = jnp.asarray(_alibi_bias())
    tri = jnp.asarray(_suffix_sum_matrix(), dtype=BF16)
    xf = x.reshape(t, d)
    w_in, w_br, w_o, w_up, w_down = (w.astype(BF16) for w in (w_in, w_br, w_o, w_up, w_down))
    for l in range(depth):
        pa, qb, kbd, vbd, pg = _in_proj(xf, norm1, w_in, l)
        oa = _mixer_a(pa, bias, batch, seq)
        ob = _mixer_b(qb, kbd, vbd, tri, batch, seq)
        xf = _merge_ffn(xf, oa, ob, pg, b_gate, w_br, w_o, norm2, w_up, conv_w, conv_b, w_down, norm_f, l, seq,
                        final_norm=(l == depth - 1))
    return xf.reshape(batch, seq, d)
```

```python
import functools

import numpy as np
import jax
import jax.numpy as jnp
from jax import lax
from jax.experimental import pallas as pl
from jax.experimental.pallas import tpu as pltpu

D_MODEL = 1024
HEAD_DIM = 64
DSW_GROUPS = ((128, 1), (512, 4), (2048, 16))
N_GROUPS = len(DSW_GROUPS)
HEADS_PER_GROUP = 4
DSW_HEADS = HEADS_PER_GROUP * N_GROUPS
SB_HEADS = 4
W_A = DSW_HEADS * HEAD_DIM
W_B = SB_HEADS * HEAD_DIM
OUT_A = HEADS_PER_GROUP * HEAD_DIM
N_IN = 3 * W_A + 3 * W_B + 2 * D_MODEL
D_FF = 2816
CONV_WIDTH = 3
RMS_EPS = 1e-6
QK_SCALE = HEAD_DIM ** -0.5

LANES = 128
WIN = 128
MASKED = -1e30

VMEM_LIMIT = 56 * 1024 * 1024

F32 = jnp.float32
BF16 = jnp.bfloat16


def _dot(a, b):
    return jnp.dot(a, b, preferred_element_type=F32)


def _dot_nt(a, b):
    return lax.dot_general(a, b, (((1,), (1,)), ((), ())), preferred_element_type=F32)


def _rms(x, g):
    ms = jnp.mean(x * x, axis=-1, keepdims=True)
    return x * lax.rsqrt(ms + RMS_EPS) * g


IN_TM = 512
IN_CHUNK = 256
B_BLK = 128
B_TQ = 256
B_STREAMS = 4
B_CAT = SB_HEADS * B_BLK
LOG2E = 1.4426950408889634
RUN_FLOOR = -160.0


def _in_proj_kernel(x_ref, g_ref, w_ref, pa_ref, qb_ref, kb_ref, vb_ref, pg_ref):
    h = _rms(x_ref[...], g_ref[...]).astype(BF16)

    def proj(base):
        return _dot(h, w_ref[:, base:base + IN_CHUNK])

    for c in range(0, 3 * W_A, IN_CHUNK):
        pa_ref[:, c:c + IN_CHUNK] = proj(c)
    base_b = 3 * W_A
    qb_ref[...] = (proj(base_b) * (-QK_SCALE * LOG2E)).astype(BF16)
    kb_ref[...] = proj(base_b + W_B).astype(BF16)
    vb_ref[...] = proj(base_b + 2 * W_B).astype(BF16)
    base_g = 3 * W_A + 3 * W_B
    for c in range(0, 2 * D_MODEL, IN_CHUNK):
        pg_ref[:, c:c + IN_CHUNK] = proj(base_g + c).astype(BF16)


def _layer_spec(layer, rows, cols, **kw):
    return pl.BlockSpec((None, rows, cols), lambda i: (layer, 0, 0), **kw)


def _in_proj(x, g, w, layer):
    t = x.shape[0]
    return pl.pallas_call(
        _in_proj_kernel,
        grid=(t // IN_TM,),
        in_specs=[
            pl.BlockSpec((IN_TM, D_MODEL), lambda i: (i, 0)),
            _layer_spec(layer, 1, D_MODEL),
            _layer_spec(layer, D_MODEL, N_IN),
        ],
        out_specs=[
            pl.BlockSpec((IN_TM, 3 * W_A), lambda i: (i, 0)),
            pl.BlockSpec((IN_TM, W_B), lambda i: (i, 0)),
            pl.BlockSpec((IN_TM, W_B), lambda i: (i, 0)),
            pl.BlockSpec((IN_TM, W_B), lambda i: (i, 0)),
            pl.BlockSpec((IN_TM, 2 * D_MODEL), lambda i: (i, 0)),
        ],
        out_shape=[
            jax.ShapeDtypeStruct((t, 3 * W_A), F32),
            jax.ShapeDtypeStruct((t, W_B), BF16),
            jax.ShapeDtypeStruct((t, W_B), BF16),
            jax.ShapeDtypeStruct((t, W_B), BF16),
            jax.ShapeDtypeStruct((t, 2 * D_MODEL), BF16),
        ],
        compiler_params=pltpu.CompilerParams(dimension_semantics=("arbitrary",), vmem_limit_bytes=VMEM_LIMIT),
        name="in_proj",
    )(x, g.reshape(-1, 1, D_MODEL), w)


A_TILE = WIN * max(d for _, d in DSW_GROUPS)


def _alibi_bias():
    slopes = 2.0 ** (-8.0 * np.arange(1, DSW_HEADS + 1) / DSW_HEADS)
    qi = np.arange(WIN)[:, None]
    kj = np.arange(2 * WIN)[None, :]
    delta = qi + WIN - kj
    valid = (delta >= 0) & (delta <= WIN)
    out = np.empty((N_GROUPS, 2, 2, 2 * WIN, 2 * WIN), np.float32)
    for g, (_, dil) in enumerate(DSW_GROUPS):
        for h in range(HEADS_PER_GROUP):
            b = -slopes[g * HEADS_PER_GROUP + h] * LOG2E * (delta * dil).astype(np.float32)
            rows = slice((h % 2) * WIN, (h % 2 + 1) * WIN)
            out[g, h // 2, 0, rows] = np.where(valid, b, MASKED)
            out[g, h // 2, 1, rows] = np.where(valid & (kj >= WIN), b, MASKED)
    return out


A_UNROLL = {1: 5, 4: 6, 16: 8}


def _mixer_a_kernel(*refs):
    q_refs = refs[0:3]
    kc_refs = refs[3:6]
    vc_refs = refs[6:9]
    kp_refs = refs[9:12]
    vp_refs = refs[12:15]
    bias_ref = refs[15]
    o_ref = refs[16]
    acc_s, m_s, l_s = refs[17:20]

    first = (pl.program_id(1) == 0).astype(jnp.int32)
    lane = lax.broadcasted_iota(jnp.int32, (WIN, LANES), 1)
    head0 = lane < HEAD_DIM
    ones = jnp.ones((2 * WIN, LANES), BF16)
    n_units = A_TILE // WIN

    for g, (_, dil) in enumerate(DSW_GROUPS):
        q_ref, kc_ref, vc_ref = q_refs[g], kc_refs[g], vc_refs[g]

        def rows(ref, start, dil=dil):
            if dil == 1:
                return ref[pl.ds(start, WIN), :]
            return ref[pl.ds(start, WIN, stride=dil), :]

        def unit(kprev_ref, vprev_ref, prev_off, off, sel, g=g, dil=dil, q_ref=q_ref, kc_ref=kc_ref,
                 vc_ref=vc_ref, rows=rows):
            q = rows(q_ref, off) * (QK_SCALE * LOG2E)
            q2 = jnp.concatenate([jnp.where(head0, q, 0.0), jnp.where(head0, 0.0, q)], axis=0).astype(BF16)
            k = jnp.concatenate([rows(kprev_ref, prev_off), rows(kc_ref, off)], axis=0).astype(BF16)
            v = jnp.concatenate([rows(vprev_ref, prev_off), rows(vc_ref, off)], axis=0).astype(BF16)
            s = _dot_nt(q2, k) + bias_ref[g, sel]
            m = jnp.max(s, axis=-1, keepdims=True)
            p = jnp.exp2(s - m).astype(BF16)
            pvl = _dot(p, jnp.concatenate([v, ones], axis=1))
            dst = pl.ds(off, WIN) if dil == 1 else pl.ds(off, WIN, stride=dil)
            acc_s[g, dst, :] = jnp.where(head0, pvl[0:WIN, 0:LANES], pvl[WIN:2 * WIN, 0:LANES])
            l_s[g, dst, :] = jnp.where(head0, pvl[0:WIN, LANES:2 * LANES], pvl[WIN:2 * WIN, LANES:2 * LANES])
            m_s[g, dst, :] = jnp.where(head0, m[0:WIN], m[WIN:2 * WIN])

        def head_unit(r, unit=unit, kp_ref=kp_refs[g], vp_ref=vp_refs[g]):
            unit(kp_ref, vp_ref, r, r, first)

        def tail_unit(idx, unit=unit, kc_ref=kc_ref, vc_ref=vc_ref, dil=dil):
            shift = dil.bit_length() - 1
            u = idx >> shift if isinstance(idx, int) else lax.shift_right_logical(idx, shift)
            off = u * (WIN * dil) + (idx & (dil - 1))
            unit(kc_ref, vc_ref, off - WIN * dil, off, 0)

        unroll = A_UNROLL[dil]
        for fn, lo, hi in ((head_unit, 0, dil), (tail_unit, dil, n_units)):
            count = hi - lo
            if count == 0:
                continue
            if count <= unroll:
                for idx in range(lo, hi):
                    fn(idx)
                continue
            assert count % unroll == 0

            def body(t, c, fn=fn, lo=lo, unroll=unroll):
                for uu in range(unroll):
                    fn(lo + t * unroll + uu)
                return c

            lax.fori_loop(0, count // unroll, body, 0)

    m_all = jnp.maximum(jnp.maximum(m_s[0], m_s[1]), m_s[2])
    num = jnp.zeros((A_TILE, LANES), F32)
    den = jnp.zeros((A_TILE, LANES), F32)
    for g in range(N_GROUPS):
        w = jnp.exp2(m_s[g] - m_all)
        num = num + w * acc_s[g]
        den = den + w * l_s[g]
    o_ref[...] = (num / den).astype(o_ref.dtype)


def _mixer_a(pa, bias, batch, seq):
    t = pa.shape[0]
    tiles = seq // A_TILE
    qcol = lambda g: (lambda b, n, hp, g=g: (b * tiles + n, 0 * 6 + 2 * g + hp))
    kcol = lambda g: (lambda b, n, hp, g=g: (b * tiles + n, 1 * 6 + 2 * g + hp))
    vcol = lambda g: (lambda b, n, hp, g=g: (b * tiles + n, 2 * 6 + 2 * g + hp))

    def prev(kind, g):
        per_tile = A_TILE // (WIN * DSW_GROUPS[g][1])
        return lambda b, n, hp: (jnp.maximum((b * tiles + n) * per_tile - 1, 0), kind * 6 + 2 * g + hp)

    in_specs = (
        [pl.BlockSpec((A_TILE, LANES), qcol(g)) for g in range(N_GROUPS)]
        + [pl.BlockSpec((A_TILE, LANES), kcol(g)) for g in range(N_GROUPS)]
        + [pl.BlockSpec((A_TILE, LANES), vcol(g)) for g in range(N_GROUPS)]
        + [pl.BlockSpec((WIN * DSW_GROUPS[g][1], LANES), prev(1, g)) for g in range(N_GROUPS)]
        + [pl.BlockSpec((WIN * DSW_GROUPS[g][1], LANES), prev(2, g)) for g in range(N_GROUPS)]
        + [pl.BlockSpec((N_GROUPS, None, 2, 2 * WIN, 2 * WIN), lambda b, n, hp: (0, hp, 0, 0, 0))]
    )
    return pl.pallas_call(
        _mixer_a_kernel,
        grid=(batch, tiles, 2),
        in_specs=in_specs,
        out_specs=pl.BlockSpec((A_TILE, LANES), lambda b, n, hp: (b * tiles + n, hp)),
        out_shape=jax.ShapeDtypeStruct((t, OUT_A), BF16),
        scratch_shapes=[pltpu.VMEM((N_GROUPS, A_TILE, LANES), F32)] * 3,
        compiler_params=pltpu.CompilerParams(
            dimension_semantics=("arbitrary", "arbitrary", "arbitrary"), vmem_limit_bytes=VMEM_LIMIT),
        name="mixer_a",
    )(*([pa] * 15), bias)


def _suffix_sum_matrix():
    j = np.arange(B_BLK)[:, None]
    s = np.arange(B_BLK)[None, :]
    later = (j > s).astype(np.float32)
    ones = np.ones((B_BLK, B_BLK), np.float32)
    zero = np.zeros((B_BLK, B_BLK), np.float32)
    return np.block([[later, zero, ones, zero], [zero, later, zero, ones]])


def _mixer_b_kernel(q_ref, k_ref, v_ref, tri_ref, o_ref, acc_s, run_s):
    step = pl.program_id(1)
    tri = tri_ref[...]
    t_loc = lax.broadcasted_iota(jnp.int32, (B_TQ, B_CAT), 0)
    s_loc = jnp.bitwise_and(lax.broadcasted_iota(jnp.int32, (B_TQ, B_CAT), 1), B_BLK - 1)
    per_q = B_TQ // B_BLK
    lane = lax.broadcasted_iota(jnp.int32, (B_BLK, W_B), 1)
    head_lanes = [(lane >= h * HEAD_DIM) & (lane < (h + 1) * HEAD_DIM) for h in range(SB_HEADS)]

    def expand(ref, j):
        blk = ref[pl.ds(pl.multiple_of(j * B_BLK, B_BLK), B_BLK), :]
        return jnp.concatenate([jnp.where(keep, blk, jnp.zeros_like(blk)) for keep in head_lanes], axis=0)

    def block(s, kv, diag_offset, first):
        zn = _dot_nt(q_ref[s * B_TQ:(s + 1) * B_TQ, :], kv[0])
        soft = jnp.log2(1.0 + jnp.exp2(-jnp.abs(zn)))
        log_stay = jnp.minimum(zn, 0.0) - soft
        if diag_offset is not None:
            causal = s_loc + diag_offset < t_loc
            log_stay = jnp.where(causal, log_stay, 0.0)
        ls16 = log_stay.astype(BF16)
        half = SB_HEADS // 2 * B_BLK
        sums = [_dot(ls16[:, p * half:(p + 1) * half], tri) for p in range(2)]
        later = jnp.concatenate([x[:, 0:half] for x in sums], axis=1)
        whole = jnp.concatenate([x[:, half:2 * half] for x in sums], axis=1)
        expo = (log_stay - zn) + later
        if not first:
            run = run_s[s]
            expo = expo + run
            whole = whole + run
        p = jnp.exp2(expo)
        if diag_offset is not None:
            p = jnp.where(causal, p, 0.0)
        run_s[s] = whole
        pv = _dot(p.astype(BF16), kv[1])
        if first:
            acc_s[s] = pv
        else:
            acc_s[s] += pv

    def straight_line(streams_with_trip):
        base = step * (B_STREAMS * per_q)
        cache = {}

        def kv(rel):
            if rel not in cache:
                cache[rel] = (expand(k_ref, base + rel), expand(v_ref, base + rel))
            return cache[rel]

        for s in range(B_STREAMS):
            for d in range(per_q - 1, -1, -1):
                block(s, kv(s * per_q + d), d * B_BLK, first=(d == per_q - 1))
            if s in streams_with_trip:
                for d in range(per_q):
                    block(s, kv(s * per_q - 1 - d), None, first=False)

    def trip(s, jj):
        i = step * B_STREAMS + s
        for d in range(per_q):
            j = (i - jj) * per_q - 1 - d
            block(s, (expand(k_ref, j), expand(v_ref, j)), None, first=False)

    @pl.when(step > 0)
    def _():
        straight_line(range(B_STREAMS))

    @pl.when(step == 0)
    def _():
        straight_line(range(1, B_STREAMS))

    def alive(s):
        run = run_s[s]
        top = run[:, 0:B_BLK]
        for h in range(1, SB_HEADS):
            top = jnp.maximum(top, run[:, h * B_BLK:(h + 1) * B_BLK])
        return jnp.max(top) > RUN_FLOOR

    for s in range(B_STREAMS):
        i = step * B_STREAMS + s

        def cond(c, i=i):
            jj, live = c
            return jnp.logical_and(jj < i, live)

        def body(c, s=s):
            jj, _ = c
            trip(s, jj)
            return jj + 1, alive(s)

        lax.while_loop(cond, body, (jnp.int32(1), alive(s)))

    o_ref[...] = acc_s[...].reshape(B_STREAMS * B_TQ, W_B).astype(o_ref.dtype)


def _mixer_b(qb, kb, vb, tri, batch, seq):
    t = qb.shape[0]
    rows = B_STREAMS * B_TQ
    whole_seq = pl.BlockSpec((None, seq, W_B), lambda b, i: (b, 0, 0))
    out = pl.pallas_call(
        _mixer_b_kernel,
        grid=(batch, seq // rows),
        in_specs=[
            pl.BlockSpec((None, rows, W_B), lambda b, i: (b, i, 0)),
            whole_seq,
            whole_seq,
            pl.BlockSpec((2 * B_BLK, 4 * B_BLK), lambda b, i: (0, 0)),
        ],
        out_specs=pl.BlockSpec((None, rows, W_B), lambda b, i: (b, i, 0)),
        out_shape=jax.ShapeDtypeStruct((batch, seq, W_B), BF16),
        scratch_shapes=[pltpu.VMEM((B_STREAMS, B_TQ, W_B), F32), pltpu.VMEM((B_STREAMS, B_TQ, B_CAT), F32)],
        compiler_params=pltpu.CompilerParams(
            dimension_semantics=("arbitrary", "arbitrary"), vmem_limit_bytes=VMEM_LIMIT),
        name="mixer_b",
    )(qb.reshape(batch, seq, W_B), kb.reshape(batch, seq, W_B), vb.reshape(batch, seq, W_B), tri)
    return out.reshape(t, W_B)


FFN_TM = 512
FFN_TF = 256
HALO = 16


def _merge_ffn_kernel(x_ref, oa_ref, ob_ref, pg_ref, bg_ref, wbr_ref, wo_ref, g_ref, wup_ref, cw_ref, cb_ref,
                      wd_ref, gf_ref, o_ref, x1_s, h_s, up_s, val_s, acc_s, *, tiles_per_seq, final_norm):
    i = pl.program_id(0)

    gates = jax.nn.sigmoid(pg_ref[...].astype(F32) + bg_ref[...])
    ya = _dot(oa_ref[...], wbr_ref[0:OUT_A, :])
    yb = _dot(ob_ref[...], wbr_ref[OUT_A:OUT_A + W_B, :])
    merged = gates[:, 0:D_MODEL] * ya + gates[:, D_MODEL:2 * D_MODEL] * yb
    x1 = x_ref[...] + _dot(merged.astype(BF16), wo_ref[...])
    x1_s[...] = x1

    @pl.when(i == 0)
    def _():
        h_s[FFN_TM:FFN_TM + HALO, :] = jnp.zeros((HALO, D_MODEL), BF16)

    h_s[0:HALO, :] = h_s[FFN_TM:FFN_TM + HALO, :]
    h_s[HALO:HALO + FFN_TM, :] = _rms(x1, g_ref[...]).astype(BF16)
    seq_start = (i % tiles_per_seq) == 0
    n_chunks = D_FF // FFN_TF

    def project(c):
        slot = c % 2
        up = _dot(h_s[...], wup_ref[:, c * FFN_TF:(c + 1) * FFN_TF])
        for ls in range(FFN_TF // LANES):
            lanes = slice(ls * LANES, (ls + 1) * LANES)
            up_s[slot, ls, pl.ds(0, HALO, stride=2), :] = jnp.where(seq_start, 0.0, up[0:HALO, lanes])
            up_s[slot, ls, pl.ds(2 * HALO, FFN_TM, stride=2), :] = up[HALO:HALO + FFN_TM, lanes]
        val_s[slot] = _dot(h_s[HALO:HALO + FFN_TM, :], wup_ref[:, D_FF + c * FFN_TF:D_FF + (c + 1) * FFN_TF])

    project(0)
    for c in range(n_chunks):
        if c + 1 < n_chunks:
            project(c + 1)
        slot = c % 2
        cols = slice(c * FFN_TF, (c + 1) * FFN_TF)
        parts = []
        for ls in range(FFN_TF // LANES):
            lanes = slice(c * FFN_TF + ls * LANES, c * FFN_TF + (ls + 1) * LANES)
            a = cb_ref[:, lanes]
            for tap in range(CONV_WIDTH):
                lag = CONV_WIDTH - 1 - tap
                a = a + up_s[slot, ls, pl.ds(2 * (HALO - lag), FFN_TM, stride=2), :] * cw_ref[tap:tap + 1, lanes]
            parts.append(a)
        a = jnp.concatenate(parts, axis=1)
        gelu = 0.5 * a * (1.0 + lax.erf(a * (2.0 ** -0.5)))
        down = _dot((gelu * val_s[slot]).astype(BF16), wd_ref[cols, :])
        if c == 0:
            acc_s[...] = down
        else:
            acc_s[...] += down

    y = x1_s[...] + acc_s[...]
    if final_norm:
        y = _rms(y, gf_ref[...])
    o_ref[...] = y


def _merge_ffn(x, oa, ob, pg, bg, wbr, wo, g, w_up, cw, cb, w_down, gf, layer, seq, final_norm):
    t = x.shape[0]
    kern = functools.partial(_merge_ffn_kernel, tiles_per_seq=seq // FFN_TM, final_norm=final_norm)
    row = lambda w: pl.BlockSpec((FFN_TM, w), lambda i: (i, 0))
    resident = lambda r, c: _layer_spec(layer, r, c, pipeline_mode=pl.Buffered(1))
    return pl.pallas_call(
        kern,
        grid=(t // FFN_TM,),
        in_specs=[
            row(D_MODEL), row(OUT_A), row(W_B), row(2 * D_MODEL),
            resident(1, 2 * D_MODEL),
            resident(OUT_A + W_B, D_MODEL),
            resident(D_MODEL, D_MODEL),
            resident(1, D_MODEL),
            resident(D_MODEL, 2 * D_FF),
            resident(CONV_WIDTH, D_FF),
            resident(1, D_FF),
            resident(D_FF, D_MODEL),
            pl.BlockSpec((1, D_MODEL), lambda i: (0, 0)),
        ],
        out_specs=row(D_MODEL),
        out_shape=jax.ShapeDtypeStruct((t, D_MODEL), F32),
        scratch_shapes=[
            pltpu.VMEM((FFN_TM, D_MODEL), F32),
            pltpu.VMEM((HALO + FFN_TM, D_MODEL), BF16),
            pltpu.VMEM((2, FFN_TF // LANES, 2 * (HALO + FFN_TM), LANES), F32),
            pltpu.VMEM((2, FFN_TM, FFN_TF), F32),
            pltpu.VMEM((FFN_TM, D_MODEL), F32),
        ],
        compiler_params=pltpu.CompilerParams(dimension_semantics=("arbitrary",), vmem_limit_bytes=VMEM_LIMIT),
        name="merge_ffn",
    )(x, oa, ob, pg, bg.reshape(-1, 1, 2 * D_MODEL), wbr, wo, g.reshape(-1, 1, D_MODEL), w_up, cw,
      cb.reshape(-1, 1, D_FF), w_down, gf.reshape(1, D_MODEL))


def kernel(x, norm1, w_in, b_gate, w_br, w_o, norm2, w_up, conv_w, conv_b, w_down, norm_f):
    batch, seq, d = x.shape
    depth = norm1.shape[0]
    assert d == D_MODEL and seq % A_TILE == 0 and seq % FFN_TM == 0
    t = batch * seq
    bias = jnp.asarray(_alibi_bias())
    tri = jnp.asarray(_suffix_sum_matrix(), dtype=BF16)
    xf = x.reshape(t, d)
    w_in, w_br, w_o, w_up, w_down = (w.astype(BF16) for w in (w_in, w_br, w_o, w_up, w_down))
    for l in range(depth):
        pa, qb, kb, vb, pg = _in_proj(xf, norm1, w_in, l)
        oa = _mixer_a(pa, bias, batch, seq)
        ob = _mixer_b(qb, kb, vb, tri, batch, seq)
        xf = _merge_ffn(xf, oa, ob, pg, b_gate, w_br, w_o, norm2, w_up, conv_w, conv_b, w_down, norm_f, l, seq,
                        final_norm=(l == depth - 1))
    return xf.reshape(batch, seq, d)
```

```python
import functools

import numpy as np
import jax
import jax.numpy as jnp
from jax import lax
from jax.experimental import pallas as pl
from jax.experimental.pallas import tpu as pltpu

D_MODEL = 1024
HEAD_DIM = 64
DSW_GROUPS = ((128, 1), (512, 4), (2048, 16))
N_GROUPS = len(DSW_GROUPS)
HEADS_PER_GROUP = 4
DSW_HEADS = HEADS_PER_GROUP * N_GROUPS
SB_HEADS = 4
W_A = DSW_HEADS * HEAD_DIM
W_B = SB_HEADS * HEAD_DIM
OUT_A = HEADS_PER_GROUP * HEAD_DIM
N_IN = 3 * W_A + 3 * W_B + 2 * D_MODEL
D_FF = 2816
CONV_WIDTH = 3
RMS_EPS = 1e-6
QK_SCALE = HEAD_DIM ** -0.5

LANES = 128
WIN = 128
MASKED = -1e30

VMEM_LIMIT = 56 * 1024 * 1024

F32 = jnp.float32
BF16 = jnp.bfloat16


def _dot(a, b):
    return jnp.dot(a, b, preferred_element_type=F32)


def _dot_nt(a, b):
    return lax.dot_general(a, b, (((1,), (1,)), ((), ())), preferred_element_type=F32)


def _rms(x, g):
    ms = jnp.mean(x * x, axis=-1, keepdims=True)
    return x * lax.rsqrt(ms + RMS_EPS) * g


IN_TM = 512
IN_CHUNK = 256
B_BLK = 128
B_TQ = 256
B_STREAMS = 4
B_CAT = SB_HEADS * B_BLK
LOG2E = 1.4426950408889634
RUN_FLOOR = -160.0


def _in_proj_kernel(x_ref, g_ref, w_ref, pa_ref, qb_ref, kb_ref, vb_ref, pg_ref):
    h = _rms(x_ref[...], g_ref[...]).astype(BF16)

    def proj(base):
        return _dot(h, w_ref[:, base:base + IN_CHUNK])

    for c in range(0, 3 * W_A, IN_CHUNK):
        pa_ref[:, c:c + IN_CHUNK] = proj(c)
    base_b = 3 * W_A
    qb_ref[...] = (proj(base_b) * (-QK_SCALE * LOG2E)).astype(BF16)
    kb_ref[...] = proj(base_b + W_B).astype(BF16)
    vb_ref[...] = proj(base_b + 2 * W_B).astype(BF16)
    base_g = 3 * W_A + 3 * W_B
    for c in range(0, 2 * D_MODEL, IN_CHUNK):
        pg_ref[:, c:c + IN_CHUNK] = proj(base_g + c).astype(BF16)


def _layer_spec(layer, rows, cols, **kw):
    return pl.BlockSpec((None, rows, cols), lambda i: (layer, 0, 0), **kw)


def _in_proj(x, g, w, layer):
    t = x.shape[0]
    return pl.pallas_call(
        _in_proj_kernel,
        grid=(t // IN_TM,),
        in_specs=[
            pl.BlockSpec((IN_TM, D_MODEL), lambda i: (i, 0)),
            _layer_spec(layer, 1, D_MODEL),
            _layer_spec(layer, D_MODEL, N_IN),
        ],
        out_specs=[
            pl.BlockSpec((IN_TM, 3 * W_A), lambda i: (i, 0)),
            pl.BlockSpec((IN_TM, W_B), lambda i: (i, 0)),
            pl.BlockSpec((IN_TM, W_B), lambda i: (i, 0)),
            pl.BlockSpec((IN_TM, W_B), lambda i: (i, 0)),
            pl.BlockSpec((IN_TM, 2 * D_MODEL), lambda i: (i, 0)),
        ],
        out_shape=[
            jax.ShapeDtypeStruct((t, 3 * W_A), F32),
            jax.ShapeDtypeStruct((t, W_B), BF16),
            jax.ShapeDtypeStruct((t, W_B), BF16),
            jax.ShapeDtypeStruct((t, W_B), BF16),
            jax.ShapeDtypeStruct((t, 2 * D_MODEL), BF16),
        ],
        compiler_params=pltpu.CompilerParams(dimension_semantics=("arbitrary",), vmem_limit_bytes=VMEM_LIMIT),
        name="in_proj",
    )(x, g.reshape(-1, 1, D_MODEL), w)


A_TILE = WIN * max(d for _, d in DSW_GROUPS)


def _alibi_bias():
    slopes = 2.0 ** (-8.0 * np.arange(1, DSW_HEADS + 1) / DSW_HEADS)
    qi = np.arange(WIN)[:, None]
    kj = np.arange(2 * WIN)[None, :]
    delta = qi + WIN - kj
    valid = (delta >= 0) & (delta <= WIN)
    out = np.empty((N_GROUPS, 2, 2, 2 * WIN, 2 * WIN), np.float32)
    for g, (_, dil) in enumerate(DSW_GROUPS):
        for h in range(HEADS_PER_GROUP):
            b = -slopes[g * HEADS_PER_GROUP + h] * LOG2E * (delta * dil).astype(np.float32)
            rows = slice((h % 2) * WIN, (h % 2 + 1) * WIN)
            out[g, h // 2, 0, rows] = np.where(valid, b, MASKED)
            out[g, h // 2, 1, rows] = np.where(valid & (kj >= WIN), b, MASKED)
    return out


A_UNROLL = {1: 15, 4: 12, 16: 16}


def _mixer_a_kernel(*refs):
    q_refs = refs[0:3]
    kc_refs = refs[3:6]
    vc_refs = refs[6:9]
    kp_refs = refs[9:12]
    vp_refs = refs[12:15]
    bias_ref = refs[15]
    o_ref = refs[16]
    acc_s, m_s, l_s = refs[17:20]

    first = (pl.program_id(1) == 0).astype(jnp.int32)
    lane = lax.broadcasted_iota(jnp.int32, (WIN, LANES), 1)
    head0 = lane < HEAD_DIM
    ones = jnp.ones((2 * WIN, LANES), BF16)
    n_units = A_TILE // WIN

    for g, (_, dil) in enumerate(DSW_GROUPS):
        q_ref, kc_ref, vc_ref = q_refs[g], kc_refs[g], vc_refs[g]

        def rows(ref, start, dil=dil):
            if dil == 1:
                return ref[pl.ds(start, WIN), :]
            return ref[pl.ds(start, WIN, stride=dil), :]

        def unit(kprev_ref, vprev_ref, prev_off, off, sel, g=g, dil=dil, q_ref=q_ref, kc_ref=kc_ref,
                 vc_ref=vc_ref, rows=rows):
            q = rows(q_ref, off) * (QK_SCALE * LOG2E)
            q2 = jnp.concatenate([jnp.where(head0, q, 0.0), jnp.where(head0, 0.0, q)], axis=0).astype(BF16)
            k = jnp.concatenate([rows(kprev_ref, prev_off), rows(kc_ref, off)], axis=0).astype(BF16)
            v = jnp.concatenate([rows(vprev_ref, prev_off), rows(vc_ref, off)], axis=0).astype(BF16)
            s = _dot_nt(q2, k) + bias_ref[g, sel]
            m = jnp.max(s, axis=-1, keepdims=True)
            p = jnp.exp2(s - m).astype(BF16)
            pvl = _dot(p, jnp.concatenate([v, ones], axis=1))
            dst = pl.ds(off, WIN) if dil == 1 else pl.ds(off, WIN, stride=dil)
            acc_s[g, dst, :] = jnp.where(head0, pvl[0:WIN, 0:LANES], pvl[WIN:2 * WIN, 0:LANES])
            l_s[g, dst, :] = jnp.where(head0, pvl[0:WIN, LANES:2 * LANES], pvl[WIN:2 * WIN, LANES:2 * LANES])
            m_s[g, dst, :] = jnp.where(head0, m[0:WIN], m[WIN:2 * WIN])

        def head_unit(r, unit=unit, kp_ref=kp_refs[g], vp_ref=vp_refs[g]):
            unit(kp_ref, vp_ref, r, r, first)

        def tail_unit(idx, unit=unit, kc_ref=kc_ref, vc_ref=vc_ref, dil=dil):
            shift = dil.bit_length() - 1
            u = idx >> shift if isinstance(idx, int) else lax.shift_right_logical(idx, shift)
            off = u * (WIN * dil) + (idx & (dil - 1))
            unit(kc_ref, vc_ref, off - WIN * dil, off, 0)

        unroll = A_UNROLL[dil]
        for fn, lo, hi in ((head_unit, 0, dil), (tail_unit, dil, n_units)):
            count = hi - lo
            if count == 0:
                continue
            if count <= unroll:
                for idx in range(lo, hi):
                    fn(idx)
                continue
            assert count % unroll == 0

            def body(t, c, fn=fn, lo=lo, unroll=unroll):
                for uu in range(unroll):
                    fn(lo + t * unroll + uu)
                return c

            lax.fori_loop(0, count // unroll, body, 0)

    m_all = jnp.maximum(jnp.maximum(m_s[0], m_s[1]), m_s[2])
    num = jnp.zeros((A_TILE, LANES), F32)
    den = jnp.zeros((A_TILE, LANES), F32)
    for g in range(N_GROUPS):
        w = jnp.exp2(m_s[g] - m_all)
        num = num + w * acc_s[g]
        den = den + w * l_s[g]
    o_ref[...] = (num / den).astype(o_ref.dtype)


def _mixer_a(pa, bias, batch, seq):
    t = pa.shape[0]
    tiles = seq // A_TILE
    qcol = lambda g: (lambda b, n, hp, g=g: (b * tiles + n, 0 * 6 + 2 * g + hp))
    kcol = lambda g: (lambda b, n, hp, g=g: (b * tiles + n, 1 * 6 + 2 * g + hp))
    vcol = lambda g: (lambda b, n, hp, g=g: (b * tiles + n, 2 * 6 + 2 * g + hp))

    def prev(kind, g):
        per_tile = A_TILE // (WIN * DSW_GROUPS[g][1])
        return lambda b, n, hp: (jnp.maximum((b * tiles + n) * per_tile - 1, 0), kind * 6 + 2 * g + hp)

    in_specs = (
        [pl.BlockSpec((A_TILE, LANES), qcol(g)) for g in range(N_GROUPS)]
        + [pl.BlockSpec((A_TILE, LANES), kcol(g)) for g in range(N_GROUPS)]
        + [pl.BlockSpec((A_TILE, LANES), vcol(g)) for g in range(N_GROUPS)]
        + [pl.BlockSpec((WIN * DSW_GROUPS[g][1], LANES), prev(1, g)) for g in range(N_GROUPS)]
        + [pl.BlockSpec((WIN * DSW_GROUPS[g][1], LANES), prev(2, g)) for g in range(N_GROUPS)]
        + [pl.BlockSpec((N_GROUPS, None, 2, 2 * WIN, 2 * WIN), lambda b, n, hp: (0, hp, 0, 0, 0))]
    )
    return pl.pallas_call(
        _mixer_a_kernel,
        grid=(batch, tiles, 2),
        in_specs=in_specs,
        out_specs=pl.BlockSpec((A_TILE, LANES), lambda b, n, hp: (b * tiles + n, hp)),
        out_shape=jax.ShapeDtypeStruct((t, OUT_A), BF16),
        scratch_shapes=[pltpu.VMEM((N_GROUPS, A_TILE, LANES), F32)] * 3,
        compiler_params=pltpu.CompilerParams(
            dimension_semantics=("arbitrary", "arbitrary", "arbitrary"), vmem_limit_bytes=VMEM_LIMIT),
        name="mixer_a",
    )(*([pa] * 15), bias)


def _suffix_sum_matrix():
    j = np.arange(B_BLK)[:, None]
    s = np.arange(B_BLK)[None, :]
    later = (j > s).astype(np.float32)
    ones = np.ones((B_BLK, B_BLK), np.float32)
    zero = np.zeros((B_BLK, B_BLK), np.float32)
    return np.block([[later, zero, ones, zero], [zero, later, zero, ones]])


def _mixer_b_kernel(q_ref, k_ref, v_ref, tri_ref, o_ref, acc_s, run_s):
    step = pl.program_id(1)
    tri = tri_ref[...]
    t_loc = lax.broadcasted_iota(jnp.int32, (B_TQ, B_CAT), 0)
    s_loc = jnp.bitwise_and(lax.broadcasted_iota(jnp.int32, (B_TQ, B_CAT), 1), B_BLK - 1)
    per_q = B_TQ // B_BLK
    lane = lax.broadcasted_iota(jnp.int32, (B_BLK, W_B), 1)
    head_lanes = [(lane >= h * HEAD_DIM) & (lane < (h + 1) * HEAD_DIM) for h in range(SB_HEADS)]

    def expand(ref, j):
        blk = ref[pl.ds(pl.multiple_of(j * B_BLK, B_BLK), B_BLK), :]
        return jnp.concatenate([jnp.where(keep, blk, jnp.zeros_like(blk)) for keep in head_lanes], axis=0)

    def block(s, kv, diag_offset, first):
        zn = _dot_nt(q_ref[s * B_TQ:(s + 1) * B_TQ, :], kv[0])
        soft = jnp.log2(1.0 + jnp.exp2(-jnp.abs(zn)))
        log_stay = jnp.minimum(zn, 0.0) - soft
        if diag_offset is not None:
            causal = s_loc + diag_offset < t_loc
            log_stay = jnp.where(causal, log_stay, 0.0)
        ls16 = log_stay.astype(BF16)
        half = SB_HEADS // 2 * B_BLK
        sums = [_dot(ls16[:, p * half:(p + 1) * half], tri) for p in range(2)]
        later = jnp.concatenate([x[:, 0:half] for x in sums], axis=1)
        whole = jnp.concatenate([x[:, half:2 * half] for x in sums], axis=1)
        expo = (log_stay - zn) + later
        if not first:
            run = run_s[s]
            expo = expo + run
            whole = whole + run
        p = jnp.exp2(expo)
        if diag_offset is not None:
            p = jnp.where(causal, p, 0.0)
        run_s[s] = whole
        pv = _dot(p.astype(BF16), kv[1])
        if first:
            acc_s[s] = pv
        else:
            acc_s[s] += pv

    def straight_line(streams_with_trip):
        base = step * (B_STREAMS * per_q)
        cache = {}

        def kv(rel):
            if rel not in cache:
                cache[rel] = (expand(k_ref, base + rel), expand(v_ref, base + rel))
            return cache[rel]

        for s in range(B_STREAMS):
            for d in range(per_q - 1, -1, -1):
                block(s, kv(s * per_q + d), d * B_BLK, first=(d == per_q - 1))
            if s in streams_with_trip:
                for d in range(per_q):
                    block(s, kv(s * per_q - 1 - d), None, first=False)

    def trip(s, jj):
        i = step * B_STREAMS + s
        for d in range(per_q):
            j = (i - jj) * per_q - 1 - d
            block(s, (expand(k_ref, j), expand(v_ref, j)), None, first=False)

    @pl.when(step > 0)
    def _():
        straight_line(range(B_STREAMS))

    @pl.when(step == 0)
    def _():
        straight_line(range(1, B_STREAMS))

    def alive(s):
        run = run_s[s]
        top = run[:, 0:B_BLK]
        for h in range(1, SB_HEADS):
            top = jnp.maximum(top, run[:, h * B_BLK:(h + 1) * B_BLK])
        return jnp.max(top) > RUN_FLOOR

    for s in range(B_STREAMS):
        i = step * B_STREAMS + s

        def cond(c, i=i):
            jj, live = c
            return jnp.logical_and(jj < i, live)

        def body(c, s=s):
            jj, _ = c
            trip(s, jj)
            return jj + 1, alive(s)

        lax.while_loop(cond, body, (jnp.int32(1), alive(s)))

    o_ref[...] = acc_s[...].reshape(B_STREAMS * B_TQ, W_B).astype(o_ref.dtype)


def _mixer_b(qb, kb, vb, tri, batch, seq):
    t = qb.shape[0]
    rows = B_STREAMS * B_TQ
    whole_seq = pl.BlockSpec((None, seq, W_B), lambda b, i: (b, 0, 0))
    out = pl.pallas_call(
        _mixer_b_kernel,
        grid=(batch, seq // rows),
        in_specs=[
            pl.BlockSpec((None, rows, W_B), lambda b, i: (b, i, 0)),
            whole_seq,
            whole_seq,
            pl.BlockSpec((2 * B_BLK, 4 * B_BLK), lambda b, i: (0, 0)),
        ],
        out_specs=pl.BlockSpec((None, rows, W_B), lambda b, i: (b, i, 0)),
        out_shape=jax.ShapeDtypeStruct((batch, seq, W_B), BF16),
        scratch_shapes=[pltpu.VMEM((B_STREAMS, B_TQ, W_B), F32), pltpu.VMEM((B_STREAMS, B_TQ, B_CAT), F32)],
        compiler_params=pltpu.CompilerParams(
            dimension_semantics=("arbitrary", "arbitrary"), vmem_limit_bytes=VMEM_LIMIT),
        name="mixer_b",
    )(qb.reshape(batch, seq, W_B), kb.reshape(batch, seq, W_B), vb.reshape(batch, seq, W_B), tri)
    return out.reshape(t, W_B)


FFN_TM = 512
FFN_TF = 256
HALO = 16


def _merge_ffn_kernel(x_ref, oa_ref, ob_ref, pg_ref, bg_ref, wbr_ref, wo_ref, g_ref, wup_ref, cw_ref, cb_ref,
                      wd_ref, gf_ref, o_ref, x1_s, h_s, up_s, val_s, acc_s, *, tiles_per_seq, final_norm):
    i = pl.program_id(0)

    gates = jax.nn.sigmoid(pg_ref[...].astype(F32) + bg_ref[...])
    ya = _dot(oa_ref[...], wbr_ref[0:OUT_A, :])
    yb = _dot(ob_ref[...], wbr_ref[OUT_A:OUT_A + W_B, :])
    merged = gates[:, 0:D_MODEL] * ya + gates[:, D_MODEL:2 * D_MODEL] * yb
    x1 = x_ref[...] + _dot(merged.astype(BF16), wo_ref[...])
    x1_s[...] = x1

    @pl.when(i == 0)
    def _():
        h_s[FFN_TM:FFN_TM + HALO, :] = jnp.zeros((HALO, D_MODEL), BF16)

    h_s[0:HALO, :] = h_s[FFN_TM:FFN_TM + HALO, :]
    h_s[HALO:HALO + FFN_TM, :] = _rms(x1, g_ref[...]).astype(BF16)
    seq_start = (i % tiles_per_seq) == 0
    n_chunks = D_FF // FFN_TF

    def project(c):
        slot = c % 2
        up = _dot(h_s[...], wup_ref[:, c * FFN_TF:(c + 1) * FFN_TF])
        for ls in range(FFN_TF // LANES):
            lanes = slice(ls * LANES, (ls + 1) * LANES)
            up_s[slot, ls, pl.ds(0, HALO, stride=2), :] = jnp.where(seq_start, 0.0, up[0:HALO, lanes])
            up_s[slot, ls, pl.ds(2 * HALO, FFN_TM, stride=2), :] = up[HALO:HALO + FFN_TM, lanes]
        val_s[slot] = _dot(h_s[HALO:HALO + FFN_TM, :], wup_ref[:, D_FF + c * FFN_TF:D_FF + (c + 1) * FFN_TF])

    project(0)
    for c in range(n_chunks):
        if c + 1 < n_chunks:
            project(c + 1)
        slot = c % 2
        cols = slice(c * FFN_TF, (c + 1) * FFN_TF)
        parts = []
        for ls in range(FFN_TF // LANES):
            lanes = slice(c * FFN_TF + ls * LANES, c * FFN_TF + (ls + 1) * LANES)
            a = cb_ref[:, lanes]
            for tap in range(CONV_WIDTH):
                lag = CONV_WIDTH - 1 - tap
                a = a + up_s[slot, ls, pl.ds(2 * (HALO - lag), FFN_TM, stride=2), :] * cw_ref[tap:tap + 1, lanes]
            parts.append(a)
        a = jnp.concatenate(parts, axis=1)
        gelu = 0.5 * a * (1.0 + lax.erf(a * (2.0 ** -0.5)))
        down = _dot((gelu * val_s[slot]).astype(BF16), wd_ref[cols, :])
        if c == 0:
            acc_s[...] = down
        else:
            acc_s[...] += down

    y = x1_s[...] + acc_s[...]
    if final_norm:
        y = _rms(y, gf_ref[...])
    o_ref[...] = y


def _merge_ffn(x, oa, ob, pg, bg, wbr, wo, g, w_up, cw, cb, w_down, gf, layer, seq, final_norm):
    t = x.shape[0]
    kern = functools.partial(_merge_ffn_kernel, tiles_per_seq=seq // FFN_TM, final_norm=final_norm)
    row = lambda w: pl.BlockSpec((FFN_TM, w), lambda i: (i, 0))
    resident = lambda r, c: _layer_spec(layer, r, c, pipeline_mode=pl.Buffered(1))
    return pl.pallas_call(
        kern,
        grid=(t // FFN_TM,),
        in_specs=[
            row(D_MODEL), row(OUT_A), row(W_B), row(2 * D_MODEL),
            resident(1, 2 * D_MODEL),
            resident(OUT_A + W_B, D_MODEL),
            resident(D_MODEL, D_MODEL),
            resident(1, D_MODEL),
            resident(D_MODEL, 2 * D_FF),
            resident(CONV_WIDTH, D_FF),
            resident(1, D_FF),
            resident(D_FF, D_MODEL),
            pl.BlockSpec((1, D_MODEL), lambda i: (0, 0)),
        ],
        out_specs=row(D_MODEL),
        out_shape=jax.ShapeDtypeStruct((t, D_MODEL), F32),
        scratch_shapes=[
            pltpu.VMEM((FFN_TM, D_MODEL), F32),
            pltpu.VMEM((HALO + FFN_TM, D_MODEL), BF16),
            pltpu.VMEM((2, FFN_TF // LANES, 2 * (HALO + FFN_TM), LANES), F32),
            pltpu.VMEM((2, FFN_TM, FFN_TF), F32),
            pltpu.VMEM((FFN_TM, D_MODEL), F32),
        ],
        compiler_params=pltpu.CompilerParams(dimension_semantics=("arbitrary",), vmem_limit_bytes=VMEM_LIMIT),
        name="merge_ffn",
    )(x, oa, ob, pg, bg.reshape(-1, 1, 2 * D_MODEL), wbr, wo, g.reshape(-1, 1, D_MODEL), w_up, cw,
      cb.reshape(-1, 1, D_FF), w_down, gf.reshape(1, D_MODEL))


def kernel(x, norm1, w_in, b_gate, w_br, w_o, norm2, w_up, conv_w, conv_b, w_down, norm_f):
    batch, seq, d = x.shape
    depth = norm1.shape[0]
    assert d == D_MODEL and seq % A_TILE == 0 and seq % FFN_TM == 0
    t = batch * seq
    bias = jnp.asarray(_alibi_bias())
    tri = jnp.asarray(_suffix_sum_matrix(), dtype=BF16)
    xf = x.reshape(t, d)
    w_in, w_br, w_o, w_up, w_down = (w.astype(BF16) for w in (w_in, w_br, w_o, w_up, w_down))
    for l in range(depth):
        pa, qb, kb, vb, pg = _in_proj(xf, norm1, w_in, l)
        oa = _mixer_a(pa, bias, batch, seq)
        ob = _mixer_b(qb, kb, vb, tri, batch, seq)
        xf = _merge_ffn(xf, oa, ob, pg, b_gate, w_br, w_o, norm2, w_up, conv_w, conv_b, w_down, norm_f, l, seq,
                        final_norm=(l == depth - 1))
    return xf.reshape(batch, seq, d)
```

```python
import functools

import numpy as np
import jax
import jax.numpy as jnp
from jax import lax
from jax.experimental import pallas as pl
from jax.experimental.pallas import tpu as pltpu

D_MODEL = 1024
HEAD_DIM = 64
DSW_GROUPS = ((128, 1), (512, 4), (2048, 16))
N_GROUPS = len(DSW_GROUPS)
HEADS_PER_GROUP = 4
DSW_HEADS = HEADS_PER_GROUP * N_GROUPS
SB_HEADS = 4
W_A = DSW_HEADS * HEAD_DIM
W_B = SB_HEADS * HEAD_DIM
OUT_A = HEADS_PER_GROUP * HEAD_DIM
N_IN = 3 * W_A + 3 * W_B + 2 * D_MODEL
D_FF = 2816
CONV_WIDTH = 3
RMS_EPS = 1e-6
QK_SCALE = HEAD_DIM ** -0.5

LANES = 128
WIN = 128
MASKED = -1e30

VMEM_LIMIT = 56 * 1024 * 1024

F32 = jnp.float32
BF16 = jnp.bfloat16


def _dot(a, b):
    return jnp.dot(a, b, preferred_element_type=F32)


def _dot_nt(a, b):
    return lax.dot_general(a, b, (((1,), (1,)), ((), ())), preferred_element_type=F32)


def _rms(x, g):
    ms = jnp.mean(x * x, axis=-1, keepdims=True)
    return x * lax.rsqrt(ms + RMS_EPS) * g


IN_TM = 512
IN_CHUNK = 256
B_BLK = 128
B_TQ = 256
B_STREAMS = 4
B_CAT = SB_HEADS * B_BLK
LOG2E = 1.4426950408889634
RUN_FLOOR = -160.0


def _in_proj_kernel(x0_ref, xn_ref, g_ref, w_ref, pa_ref, qb_ref, kb_ref, vb_ref, pg_ref, h_s):
    @pl.when(pl.program_id(0) == 0)
    def _():
        h_s[...] = _rms(x0_ref[...], g_ref[...]).astype(BF16)

    h = h_s[...]
    h_next = _rms(xn_ref[...], g_ref[...]).astype(BF16)

    def proj(base):
        return _dot(h, w_ref[:, base:base + IN_CHUNK])

    for c in range(0, 3 * W_A, IN_CHUNK):
        pa_ref[:, c:c + IN_CHUNK] = proj(c)
    base_b = 3 * W_A
    qb_ref[...] = (proj(base_b) * (-QK_SCALE * LOG2E)).astype(BF16)
    kb_ref[...] = proj(base_b + W_B).astype(BF16)
    vb_ref[...] = proj(base_b + 2 * W_B).astype(BF16)
    base_g = 3 * W_A + 3 * W_B
    for c in range(0, 2 * D_MODEL, IN_CHUNK):
        pg_ref[:, c:c + IN_CHUNK] = proj(base_g + c).astype(BF16)
    h_s[...] = h_next


def _layer_spec(layer, rows, cols, **kw):
    return pl.BlockSpec((None, rows, cols), lambda i: (layer, 0, 0), **kw)


def _in_proj(x, g, w, layer):
    t = x.shape[0]
    last = t // IN_TM - 1
    return pl.pallas_call(
        _in_proj_kernel,
        grid=(t // IN_TM,),
        in_specs=[
            pl.BlockSpec((IN_TM, D_MODEL), lambda i: (0, 0), pipeline_mode=pl.Buffered(1)),
            pl.BlockSpec((IN_TM, D_MODEL), lambda i: (jnp.minimum(i + 1, last), 0)),
            _layer_spec(layer, 1, D_MODEL),
            _layer_spec(layer, D_MODEL, N_IN),
        ],
        out_specs=[
            pl.BlockSpec((IN_TM, 3 * W_A), lambda i: (i, 0)),
            pl.BlockSpec((IN_TM, W_B), lambda i: (i, 0)),
            pl.BlockSpec((IN_TM, W_B), lambda i: (i, 0)),
            pl.BlockSpec((IN_TM, W_B), lambda i: (i, 0)),
            pl.BlockSpec((IN_TM, 2 * D_MODEL), lambda i: (i, 0)),
        ],
        out_shape=[
            jax.ShapeDtypeStruct((t, 3 * W_A), F32),
            jax.ShapeDtypeStruct((t, W_B), BF16),
            jax.ShapeDtypeStruct((t, W_B), BF16),
            jax.ShapeDtypeStruct((t, W_B), BF16),
            jax.ShapeDtypeStruct((t, 2 * D_MODEL), BF16),
        ],
        scratch_shapes=[pltpu.VMEM((IN_TM, D_MODEL), BF16)],
        compiler_params=pltpu.CompilerParams(dimension_semantics=("arbitrary",), vmem_limit_bytes=VMEM_LIMIT),
        name="in_proj",
    )(x, x, g.reshape(-1, 1, D_MODEL), w)


A_TILE = WIN * max(d for _, d in DSW_GROUPS)


def _alibi_bias():
    slopes = 2.0 ** (-8.0 * np.arange(1, DSW_HEADS + 1) / DSW_HEADS)
    qi = np.arange(WIN)[:, None]
    kj = np.arange(2 * WIN)[None, :]
    delta = qi + WIN - kj
    valid = (delta >= 0) & (delta <= WIN)
    out = np.empty((N_GROUPS, 2, 2, 2 * WIN, 2 * WIN), np.float32)
    for g, (_, dil) in enumerate(DSW_GROUPS):
        for h in range(HEADS_PER_GROUP):
            b = -slopes[g * HEADS_PER_GROUP + h] * LOG2E * (delta * dil).astype(np.float32)
            rows = slice((h % 2) * WIN, (h % 2 + 1) * WIN)
            out[g, h // 2, 0, rows] = np.where(valid, b, MASKED)
            out[g, h // 2, 1, rows] = np.where(valid & (kj >= WIN), b, MASKED)
    return out


A_UNROLL = {1: 15, 4: 12, 16: 16}


def _mixer_a_kernel(*refs):
    q_refs = refs[0:3]
    kc_refs = refs[3:6]
    vc_refs = refs[6:9]
    kp_refs = refs[9:12]
    vp_refs = refs[12:15]
    bias_ref = refs[15]
    o_ref = refs[16]
    acc_s, m_s, l_s = refs[17:20]

    first = (pl.program_id(1) == 0).astype(jnp.int32)
    lane = lax.broadcasted_iota(jnp.int32, (WIN, LANES), 1)
    head0 = lane < HEAD_DIM
    ones = jnp.ones((2 * WIN, LANES), BF16)
    n_units = A_TILE // WIN

    for g, (_, dil) in enumerate(DSW_GROUPS):
        q_ref, kc_ref, vc_ref = q_refs[g], kc_refs[g], vc_refs[g]

        def rows(ref, start, dil=dil):
            if dil == 1:
                return ref[pl.ds(start, WIN), :]
            return ref[pl.ds(start, WIN, stride=dil), :]

        def unit(kprev_ref, vprev_ref, prev_off, off, sel, g=g, dil=dil, q_ref=q_ref, kc_ref=kc_ref,
                 vc_ref=vc_ref, rows=rows):
            q = rows(q_ref, off) * (QK_SCALE * LOG2E)
            q2 = jnp.concatenate([jnp.where(head0, q, 0.0), jnp.where(head0, 0.0, q)], axis=0).astype(BF16)
            k = jnp.concatenate([rows(kprev_ref, prev_off), rows(kc_ref, off)], axis=0).astype(BF16)
            v = jnp.concatenate([rows(vprev_ref, prev_off), rows(vc_ref, off)], axis=0).astype(BF16)
            s = _dot_nt(q2, k) + bias_ref[g, sel]
            m = jnp.max(s, axis=-1, keepdims=True)
            p = jnp.exp2(s - m).astype(BF16)
            pvl = _dot(p, jnp.concatenate([v, ones], axis=1))
            dst = pl.ds(off, WIN) if dil == 1 else pl.ds(off, WIN, stride=dil)
            acc_s[g, dst, :] = jnp.where(head0, pvl[0:WIN, 0:LANES], pvl[WIN:2 * WIN, 0:LANES])
            l_s[g, dst, :] = jnp.where(head0, pvl[0:WIN, LANES:2 * LANES], pvl[WIN:2 * WIN, LANES:2 * LANES])
            m_s[g, dst, :] = jnp.where(head0, m[0:WIN], m[WIN:2 * WIN])

        def head_unit(r, unit=unit, kp_ref=kp_refs[g], vp_ref=vp_refs[g]):
            unit(kp_ref, vp_ref, r, r, first)

        def tail_unit(idx, unit=unit, kc_ref=kc_ref, vc_ref=vc_ref, dil=dil):
            shift = dil.bit_length() - 1
            u = idx >> shift if isinstance(idx, int) else lax.shift_right_logical(idx, shift)
            off = u * (WIN * dil) + (idx & (dil - 1))
            unit(kc_ref, vc_ref, off - WIN * dil, off, 0)

        unroll = A_UNROLL[dil]
        for fn, lo, hi in ((head_unit, 0, dil), (tail_unit, dil, n_units)):
            count = hi - lo
            if count == 0:
                continue
            if count <= unroll:
                for idx in range(lo, hi):
                    fn(idx)
                continue
            assert count % unroll == 0

            def body(t, c, fn=fn, lo=lo, unroll=unroll):
                for uu in range(unroll):
                    fn(lo + t * unroll + uu)
                return c

            lax.fori_loop(0, count // unroll, body, 0)

    m_all = jnp.maximum(jnp.maximum(m_s[0], m_s[1]), m_s[2])
    num = jnp.zeros((A_TILE, LANES), F32)
    den = jnp.zeros((A_TILE, LANES), F32)
    for g in range(N_GROUPS):
        w = jnp.exp2(m_s[g] - m_all)
        num = num + w * acc_s[g]
        den = den + w * l_s[g]
    o_ref[...] = (num / den).astype(o_ref.dtype)


def _mixer_a(pa, bias, batch, seq):
    t = pa.shape[0]
    tiles = seq // A_TILE
    qcol = lambda g: (lambda b, n, hp, g=g: (b * tiles + n, 0 * 6 + 2 * g + hp))
    kcol = lambda g: (lambda b, n, hp, g=g: (b * tiles + n, 1 * 6 + 2 * g + hp))
    vcol = lambda g: (lambda b, n, hp, g=g: (b * tiles + n, 2 * 6 + 2 * g + hp))

    def prev(kind, g):
        per_tile = A_TILE // (WIN * DSW_GROUPS[g][1])
        return lambda b, n, hp: (jnp.maximum((b * tiles + n) * per_tile - 1, 0), kind * 6 + 2 * g + hp)

    in_specs = (
        [pl.BlockSpec((A_TILE, LANES), qcol(g)) for g in range(N_GROUPS)]
        + [pl.BlockSpec((A_TILE, LANES), kcol(g)) for g in range(N_GROUPS)]
        + [pl.BlockSpec((A_TILE, LANES), vcol(g)) for g in range(N_GROUPS)]
        + [pl.BlockSpec((WIN * DSW_GROUPS[g][1], LANES), prev(1, g)) for g in range(N_GROUPS)]
        + [pl.BlockSpec((WIN * DSW_GROUPS[g][1], LANES), prev(2, g)) for g in range(N_GROUPS)]
        + [pl.BlockSpec((N_GROUPS, None, 2, 2 * WIN, 2 * WIN), lambda b, n, hp: (0, hp, 0, 0, 0))]
    )
    return pl.pallas_call(
        _mixer_a_kernel,
        grid=(batch, tiles, 2),
        in_specs=in_specs,
        out_specs=pl.BlockSpec((A_TILE, LANES), lambda b, n, hp: (b * tiles + n, hp)),
        out_shape=jax.ShapeDtypeStruct((t, OUT_A), BF16),
        scratch_shapes=[pltpu.VMEM((N_GROUPS, A_TILE, LANES), F32)] * 3,
        compiler_params=pltpu.CompilerParams(
            dimension_semantics=("arbitrary", "arbitrary", "arbitrary"), vmem_limit_bytes=VMEM_LIMIT),
        name="mixer_a",
    )(*([pa] * 15), bias)


def _suffix_sum_matrix():
    j = np.arange(B_BLK)[:, None]
    s = np.arange(B_BLK)[None, :]
    later = (j > s).astype(np.float32)
    ones = np.ones((B_BLK, B_BLK), np.float32)
    zero = np.zeros((B_BLK, B_BLK), np.float32)
    return np.block([[later, zero, ones, zero], [zero, later, zero, ones]])


def _mixer_b_kernel(q_ref, k_ref, v_ref, tri_ref, o_ref, acc_s, run_s):
    step = pl.program_id(1)
    tri = tri_ref[...]
    t_loc = lax.broadcasted_iota(jnp.int32, (B_TQ, B_CAT), 0)
    s_loc = jnp.bitwise_and(lax.broadcasted_iota(jnp.int32, (B_TQ, B_CAT), 1), B_BLK - 1)
    per_q = B_TQ // B_BLK
    lane = lax.broadcasted_iota(jnp.int32, (B_BLK, W_B), 1)
    head_lanes = [(lane >= h * HEAD_DIM) & (lane < (h + 1) * HEAD_DIM) for h in range(SB_HEADS)]

    def expand(ref, j):
        blk = ref[pl.ds(pl.multiple_of(j * B_BLK, B_BLK), B_BLK), :]
        return jnp.concatenate([jnp.where(keep, blk, jnp.zeros_like(blk)) for keep in head_lanes], axis=0)

    def block(s, kv, diag_offset, first):
        zn = _dot_nt(q_ref[s * B_TQ:(s + 1) * B_TQ, :], kv[0])
        soft = jnp.log2(1.0 + jnp.exp2(-jnp.abs(zn)))
        log_stay = jnp.minimum(zn, 0.0) - soft
        if diag_offset is not None:
            causal = s_loc + diag_offset < t_loc
            log_stay = jnp.where(causal, log_stay, 0.0)
        ls16 = log_stay.astype(BF16)
        half = SB_HEADS // 2 * B_BLK
        sums = [_dot(ls16[:, p * half:(p + 1) * half], tri) for p in range(2)]
        later = jnp.concatenate([x[:, 0:half] for x in sums], axis=1)
        whole = jnp.concatenate([x[:, half:2 * half] for x in sums], axis=1)
        expo = (log_stay - zn) + later
        if not first:
            run = run_s[s]
            expo = expo + run
            whole = whole + run
        p = jnp.exp2(expo)
        if diag_offset is not None:
            p = jnp.where(causal, p, 0.0)
        run_s[s] = whole
        pv = _dot(p.astype(BF16), kv[1])
        if first:
            acc_s[s] = pv
        else:
            acc_s[s] += pv

    def straight_line(streams_with_trip):
        base = step * (B_STREAMS * per_q)
        cache = {}

        def kv(rel):
            if rel not in cache:
                cache[rel] = (expand(k_ref, base + rel), expand(v_ref, base + rel))
            return cache[rel]

        for s in range(B_STREAMS):
            for d in range(per_q - 1, -1, -1):
                block(s, kv(s * per_q + d), d * B_BLK, first=(d == per_q - 1))
            if s in streams_with_trip:
                for d in range(per_q):
                    block(s, kv(s * per_q - 1 - d), None, first=False)

    def trip(s, jj):
        i = step * B_STREAMS + s
        for d in range(per_q):
            j = (i - jj) * per_q - 1 - d
            block(s, (expand(k_ref, j), expand(v_ref, j)), None, first=False)

    @pl.when(step > 0)
    def _():
        straight_line(range(B_STREAMS))

    @pl.when(step == 0)
    def _():
        straight_line(range(1, B_STREAMS))

    def alive(s):
        run = run_s[s]
        top = run[:, 0:B_BLK]
        for h in range(1, SB_HEADS):
            top = jnp.maximum(top, run[:, h * B_BLK:(h + 1) * B_BLK])
        return jnp.max(top) > RUN_FLOOR

    live0 = [alive(s) for s in range(B_STREAMS)]
    for s in range(B_STREAMS):
        i = step * B_STREAMS + s

        def cond(c, i=i):
            jj, live = c
            return jnp.logical_and(jj < i, live)

        def body(c, s=s):
            jj, _ = c
            trip(s, jj)
            return jj + 1, alive(s)

        lax.while_loop(cond, body, (jnp.int32(1), live0[s]))

    o_ref[...] = acc_s[...].reshape(B_STREAMS * B_TQ, W_B).astype(o_ref.dtype)


def _mixer_b(qb, kb, vb, tri, batch, seq):
    t = qb.shape[0]
    rows = B_STREAMS * B_TQ
    whole_seq = pl.BlockSpec((None, seq, W_B), lambda b, i: (b, 0, 0))
    out = pl.pallas_call(
        _mixer_b_kernel,
        grid=(batch, seq // rows),
        in_specs=[
            pl.BlockSpec((None, rows, W_B), lambda b, i: (b, i, 0)),
            whole_seq,
            whole_seq,
            pl.BlockSpec((2 * B_BLK, 4 * B_BLK), lambda b, i: (0, 0)),
        ],
        out_specs=pl.BlockSpec((None, rows, W_B), lambda b, i: (b, i, 0)),
        out_shape=jax.ShapeDtypeStruct((batch, seq, W_B), BF16),
        scratch_shapes=[pltpu.VMEM((B_STREAMS, B_TQ, W_B), F32), pltpu.VMEM((B_STREAMS, B_TQ, B_CAT), F32)],
        compiler_params=pltpu.CompilerParams(
            dimension_semantics=("arbitrary", "arbitrary"), vmem_limit_bytes=VMEM_LIMIT),
        name="mixer_b",
    )(qb.reshape(batch, seq, W_B), kb.reshape(batch, seq, W_B), vb.reshape(batch, seq, W_B), tri)
    return out.reshape(t, W_B)


FFN_TM = 512
FFN_TF = 256
HALO = 16


def _merge_ffn_kernel(x_ref, oa_ref, ob_ref, pg_ref, bg_ref, wbr_ref, wo_ref, g_ref, wup_ref, cw_ref, cb_ref,
                      wd_ref, gf_ref, o_ref, x1_s, h_s, up_s, val_s, acc_s, *, tiles_per_seq, final_norm):
    i = pl.program_id(0)

    gates = jax.nn.sigmoid(pg_ref[...].astype(F32) + bg_ref[...])
    ya = _dot(oa_ref[...], wbr_ref[0:OUT_A, :])
    yb = _dot(ob_ref[...], wbr_ref[OUT_A:OUT_A + W_B, :])
    merged = gates[:, 0:D_MODEL] * ya + gates[:, D_MODEL:2 * D_MODEL] * yb
    x1 = x_ref[...] + _dot(merged.astype(BF16), wo_ref[...])
    x1_s[...] = x1

    @pl.when(i == 0)
    def _():
        h_s[FFN_TM:FFN_TM + HALO, :] = jnp.zeros((HALO, D_MODEL), BF16)

    h_s[0:HALO, :] = h_s[FFN_TM:FFN_TM + HALO, :]
    h_s[HALO:HALO + FFN_TM, :] = _rms(x1, g_ref[...]).astype(BF16)
    seq_start = (i % tiles_per_seq) == 0
    n_chunks = D_FF // FFN_TF

    def project(c):
        slot = c % 2
        up = _dot(h_s[...], wup_ref[:, c * FFN_TF:(c + 1) * FFN_TF])
        for ls in range(FFN_TF // LANES):
            lanes = slice(ls * LANES, (ls + 1) * LANES)
            up_s[slot, ls, pl.ds(0, HALO, stride=2), :] = jnp.where(seq_start, 0.0, up[0:HALO, lanes])
            up_s[slot, ls, pl.ds(2 * HALO, FFN_TM, stride=2), :] = up[HALO:HALO + FFN_TM, lanes]
        val_s[slot] = _dot(h_s[HALO:HALO + FFN_TM, :], wup_ref[:, D_FF + c * FFN_TF:D_FF + (c + 1) * FFN_TF])

    project(0)
    for c in range(n_chunks):
        if c + 1 < n_chunks:
            project(c + 1)
        slot = c % 2
        cols = slice(c * FFN_TF, (c + 1) * FFN_TF)
        parts = []
        for ls in range(FFN_TF // LANES):
            lanes = slice(c * FFN_TF + ls * LANES, c * FFN_TF + (ls + 1) * LANES)
            a = cb_ref[:, lanes]
            for tap in range(CONV_WIDTH):
                lag = CONV_WIDTH - 1 - tap
                a = a + up_s[slot, ls, pl.ds(2 * (HALO - lag), FFN_TM, stride=2), :] * cw_ref[tap:tap + 1, lanes]
            parts.append(a)
        a = jnp.concatenate(parts, axis=1)
        gelu = 0.5 * a * (1.0 + lax.erf(a * (2.0 ** -0.5)))
        down = _dot((gelu * val_s[slot]).astype(BF16), wd_ref[cols, :])
        if c == 0:
            acc_s[...] = down
        else:
            acc_s[...] += down

    y = x1_s[...] + acc_s[...]
    if final_norm:
        y = _rms(y, gf_ref[...])
    o_ref[...] = y


def _merge_ffn(x, oa, ob, pg, bg, wbr, wo, g, w_up, cw, cb, w_down, gf, layer, seq, final_norm):
    t = x.shape[0]
    kern = functools.partial(_merge_ffn_kernel, tiles_per_seq=seq // FFN_TM, final_norm=final_norm)
    row = lambda w: pl.BlockSpec((FFN_TM, w), lambda i: (i, 0))
    resident = lambda r, c: _layer_spec(layer, r, c, pipeline_mode=pl.Buffered(1))
    return pl.pallas_call(
        kern,
        grid=(t // FFN_TM,),
        in_specs=[
            row(D_MODEL), row(OUT_A), row(W_B), row(2 * D_MODEL),
            resident(1, 2 * D_MODEL),
            resident(OUT_A + W_B, D_MODEL),
            resident(D_MODEL, D_MODEL),
            resident(1, D_MODEL),
            resident(D_MODEL, 2 * D_FF),
            resident(CONV_WIDTH, D_FF),
            resident(1, D_FF),
            resident(D_FF, D_MODEL),
            pl.BlockSpec((1, D_MODEL), lambda i: (0, 0)),
        ],
        out_specs=row(D_MODEL),
        out_shape=jax.ShapeDtypeStruct((t, D_MODEL), F32),
        scratch_shapes=[
            pltpu.VMEM((FFN_TM, D_MODEL), F32),
            pltpu.VMEM((HALO + FFN_TM, D_MODEL), BF16),
            pltpu.VMEM((2, FFN_TF // LANES, 2 * (HALO + FFN_TM), LANES), F32),
            pltpu.VMEM((2, FFN_TM, FFN_TF), F32),
            pltpu.VMEM((FFN_TM, D_MODEL), F32),
        ],
        compiler_params=pltpu.CompilerParams(dimension_semantics=("arbitrary",), vmem_limit_bytes=VMEM_LIMIT),
        name="merge_ffn",
    )(x, oa, ob, pg, bg.reshape(-1, 1, 2 * D_MODEL), wbr, wo, g.reshape(-1, 1, D_MODEL), w_up, cw,
      cb.reshape(-1, 1, D_FF), w_down, gf.reshape(1, D_MODEL))


def kernel(x, norm1, w_in, b_gate, w_br, w_o, norm2, w_up, conv_w, conv_b, w_down, norm_f):
    batch, seq, d = x.shape
    depth = norm1.shape[0]
    assert d == D_MODEL and seq % A_TILE == 0 and seq % FFN_TM == 0
    t = batch * seq
    bias = jnp.asarray(_alibi_bias())
    tri = jnp.asarray(_suffix_sum_matrix(), dtype=BF16)
    xf = x.reshape(t, d)
    w_in, w_br, w_o, w_up, w_down = (w.astype(BF16) for w in (w_in, w_br, w_o, w_up, w_down))
    for l in range(depth):
        pa, qb, kb, vb, pg = _in_proj(xf, norm1, w_in, l)
        oa = _mixer_a(pa, bias, batch, seq)
        ob = _mixer_b(qb, kb, vb, tri, batch, seq)
        xf = _merge_ffn(xf, oa, ob, pg, b_gate, w_br, w_o, norm2, w_up, conv_w, conv_b, w_down, norm_f, l, seq,
                        final_norm=(l == depth - 1))
    return xf.reshape(batch, seq, d)
```

```python
import functools

import numpy as np
import jax
import jax.numpy as jnp
from jax import lax
from jax.experimental import pallas as pl
from jax.experimental.pallas import tpu as pltpu

D_MODEL = 1024
HEAD_DIM = 64
DSW_GROUPS = ((128, 1), (512, 4), (2048, 16))
N_GROUPS = len(DSW_GROUPS)
HEADS_PER_GROUP = 4
DSW_HEADS = HEADS_PER_GROUP * N_GROUPS
SB_HEADS = 4
W_A = DSW_HEADS * HEAD_DIM
W_B = SB_HEADS * HEAD_DIM
OUT_A = HEADS_PER_GROUP * HEAD_DIM
N_IN = 3 * W_A + 3 * W_B + 2 * D_MODEL
D_FF = 2816
CONV_WIDTH = 3
RMS_EPS = 1e-6
QK_SCALE = HEAD_DIM ** -0.5

LANES = 128
WIN = 128
MASKED = -1e30

VMEM_LIMIT = 56 * 1024 * 1024

F32 = jnp.float32
BF16 = jnp.bfloat16


def _dot(a, b):
    return jnp.dot(a, b, preferred_element_type=F32)


def _dot_nt(a, b):
    return lax.dot_general(a, b, (((1,), (1,)), ((), ())), preferred_element_type=F32)


def _rms(x, g):
    ms = jnp.mean(x * x, axis=-1, keepdims=True)
    return x * lax.rsqrt(ms + RMS_EPS) * g


IN_TM = 512
IN_CHUNK = 256
B_BLK = 128
B_TQ = 256
B_STREAMS = 8
B_CAT = SB_HEADS * B_BLK
LOG2E = 1.4426950408889634
RUN_FLOOR = -160.0


def _in_proj_kernel(x0_ref, xn_ref, g_ref, w_ref, pa_ref, qb_ref, kb_ref, vb_ref, pg_ref, h_s):
    @pl.when(pl.program_id(0) == 0)
    def _():
        h_s[...] = _rms(x0_ref[...], g_ref[...]).astype(BF16)

    h = h_s[...]
    h_next = _rms(xn_ref[...], g_ref[...]).astype(BF16)

    def proj(base):
        return _dot(h, w_ref[:, base:base + IN_CHUNK])

    for c in range(0, 3 * W_A, IN_CHUNK):
        pa_ref[:, c:c + IN_CHUNK] = proj(c)
    base_b = 3 * W_A
    qb_ref[...] = (proj(base_b) * (-QK_SCALE * LOG2E)).astype(BF16)
    kb_ref[...] = proj(base_b + W_B).astype(BF16)
    vb_ref[...] = proj(base_b + 2 * W_B).astype(BF16)
    base_g = 3 * W_A + 3 * W_B
    for c in range(0, 2 * D_MODEL, IN_CHUNK):
        pg_ref[:, c:c + IN_CHUNK] = proj(base_g + c).astype(BF16)
    h_s[...] = h_next


def _layer_spec(layer, rows, cols, **kw):
    return pl.BlockSpec((None, rows, cols), lambda i: (layer, 0, 0), **kw)


def _in_proj(x, g, w, layer):
    t = x.shape[0]
    last = t // IN_TM - 1
    return pl.pallas_call(
        _in_proj_kernel,
        grid=(t // IN_TM,),
        in_specs=[
            pl.BlockSpec((IN_TM, D_MODEL), lambda i: (0, 0), pipeline_mode=pl.Buffered(1)),
            pl.BlockSpec((IN_TM, D_MODEL), lambda i: (jnp.minimum(i + 1, last), 0)),
            _layer_spec(layer, 1, D_MODEL),
            _layer_spec(layer, D_MODEL, N_IN),
        ],
        out_specs=[
            pl.BlockSpec((IN_TM, 3 * W_A), lambda i: (i, 0)),
            pl.BlockSpec((IN_TM, W_B), lambda i: (i, 0)),
            pl.BlockSpec((IN_TM, W_B), lambda i: (i, 0)),
            pl.BlockSpec((IN_TM, W_B), lambda i: (i, 0)),
            pl.BlockSpec((IN_TM, 2 * D_MODEL), lambda i: (i, 0)),
        ],
        out_shape=[
            jax.ShapeDtypeStruct((t, 3 * W_A), F32),
            jax.ShapeDtypeStruct((t, W_B), BF16),
            jax.ShapeDtypeStruct((t, W_B), BF16),
            jax.ShapeDtypeStruct((t, W_B), BF16),
            jax.ShapeDtypeStruct((t, 2 * D_MODEL), BF16),
        ],
        scratch_shapes=[pltpu.VMEM((IN_TM, D_MODEL), BF16)],
        compiler_params=pltpu.CompilerParams(dimension_semantics=("arbitrary",), vmem_limit_bytes=VMEM_LIMIT),
        name="in_proj",
    )(x, x, g.reshape(-1, 1, D_MODEL), w)


A_TILE = WIN * max(d for _, d in DSW_GROUPS)


def _alibi_bias():
    slopes = 2.0 ** (-8.0 * np.arange(1, DSW_HEADS + 1) / DSW_HEADS)
    qi = np.arange(WIN)[:, None]
    kj = np.arange(2 * WIN)[None, :]
    delta = qi + WIN - kj
    valid = (delta >= 0) & (delta <= WIN)
    out = np.empty((N_GROUPS, 2, 2, 2 * WIN, 2 * WIN), np.float32)
    for g, (_, dil) in enumerate(DSW_GROUPS):
        for h in range(HEADS_PER_GROUP):
            b = -slopes[g * HEADS_PER_GROUP + h] * LOG2E * (delta * dil).astype(np.float32)
            rows = slice((h % 2) * WIN, (h % 2 + 1) * WIN)
            out[g, h // 2, 0, rows] = np.where(valid, b, MASKED)
            out[g, h // 2, 1, rows] = np.where(valid & (kj >= WIN), b, MASKED)
    return out


A_MAX_STRIDE = 4
A_UNROLL = {1: 15, 4: 12, 16: 16}


def _mixer_a_kernel(*refs):
    q_refs = refs[0:3]
    kc_refs = refs[3:6]
    vc_refs = refs[6:9]
    kp_refs = refs[9:12]
    vp_refs = refs[12:15]
    bias_ref = refs[15]
    o_ref = refs[16]
    acc_s, m_s, l_s, stage_s = refs[17:21]

    first = (pl.program_id(1) == 0).astype(jnp.int32)
    lane = lax.broadcasted_iota(jnp.int32, (WIN, LANES), 1)
    head0 = lane < HEAD_DIM
    ones = jnp.ones((2 * WIN, LANES), BF16)
    n_units = A_TILE // WIN

    for g, (_, dil) in enumerate(DSW_GROUPS):
        q_ref, kc_ref, vc_ref = q_refs[g], kc_refs[g], vc_refs[g]

        staged = {}
        span = A_TILE // A_MAX_STRIDE
        if dil > A_MAX_STRIDE:
            for slot, ref in enumerate((q_ref, kc_ref, vc_ref, kp_refs[g], vp_refs[g])):
                for r in range(A_MAX_STRIDE):
                    stage_s[slot, r * span:(r + 1) * span, :] = ref[pl.ds(r, span, stride=A_MAX_STRIDE), :]
                staged[id(ref)] = slot

        def rows(ref, start, dil=dil, staged=staged, span=span):
            if dil == 1:
                return ref[pl.ds(start, WIN), :]
            if id(ref) in staged:
                first_row = (start % A_MAX_STRIDE) * span + start // A_MAX_STRIDE
                return stage_s[staged[id(ref)], pl.ds(first_row, WIN, stride=dil // A_MAX_STRIDE), :]
            return ref[pl.ds(start, WIN, stride=dil), :]

        def unit(kprev_ref, vprev_ref, prev_off, off, sel, g=g, dil=dil, q_ref=q_ref, kc_ref=kc_ref,
                 vc_ref=vc_ref, rows=rows):
            q = rows(q_ref, off) * (QK_SCALE * LOG2E)
            q2 = jnp.concatenate([jnp.where(head0, q, 0.0), jnp.where(head0, 0.0, q)], axis=0).astype(BF16)
            k = jnp.concatenate([rows(kprev_ref, prev_off), rows(kc_ref, off)], axis=0).astype(BF16)
            v = jnp.concatenate([rows(vprev_ref, prev_off), rows(vc_ref, off)], axis=0).astype(BF16)
            s = _dot_nt(q2, k) + bias_ref[g, sel]
            m = jnp.max(s, axis=-1, keepdims=True)
            p = jnp.exp2(s - m).astype(BF16)
            pvl = _dot(p, jnp.concatenate([v, ones], axis=1))
            dst = pl.ds(off, WIN) if dil == 1 else pl.ds(off, WIN, stride=dil)
            acc_s[g, dst, :] = jnp.where(head0, pvl[0:WIN, 0:LANES], pvl[WIN:2 * WIN, 0:LANES])
            l_s[g, dst, :] = jnp.where(head0, pvl[0:WIN, LANES:2 * LANES], pvl[WIN:2 * WIN, LANES:2 * LANES])
            m_s[g, dst, :] = jnp.where(head0, m[0:WIN], m[WIN:2 * WIN])

        def head_unit(r, unit=unit, kp_ref=kp_refs[g], vp_ref=vp_refs[g]):
            unit(kp_ref, vp_ref, r, r, first)

        def tail_unit(idx, unit=unit, kc_ref=kc_ref, vc_ref=vc_ref, dil=dil):
            shift = dil.bit_length() - 1
            u = idx >> shift if isinstance(idx, int) else lax.shift_right_logical(idx, shift)
            off = u * (WIN * dil) + (idx & (dil - 1))
            unit(kc_ref, vc_ref, off - WIN * dil, off, 0)

        unroll = A_UNROLL[dil]
        for fn, lo, hi in ((head_unit, 0, dil), (tail_unit, dil, n_units)):
            count = hi - lo
            if count == 0:
                continue
            if count <= unroll:
                for idx in range(lo, hi):
                    fn(idx)
                continue
            assert count % unroll == 0

            def body(t, c, fn=fn, lo=lo, unroll=unroll):
                for uu in range(unroll):
                    fn(lo + t * unroll + uu)
                return c

            lax.fori_loop(0, count // unroll, body, 0)

    m_all = jnp.maximum(jnp.maximum(m_s[0], m_s[1]), m_s[2])
    num = jnp.zeros((A_TILE, LANES), F32)
    den = jnp.zeros((A_TILE, LANES), F32)
    for g in range(N_GROUPS):
        w = jnp.exp2(m_s[g] - m_all)
        num = num + w * acc_s[g]
        den = den + w * l_s[g]
    o_ref[...] = (num / den).astype(o_ref.dtype)


def _mixer_a(pa, bias, batch, seq):
    t = pa.shape[0]
    tiles = seq // A_TILE
    qcol = lambda g: (lambda b, n, hp, g=g: (b * tiles + n, 0 * 6 + 2 * g + hp))
    kcol = lambda g: (lambda b, n, hp, g=g: (b * tiles + n, 1 * 6 + 2 * g + hp))
    vcol = lambda g: (lambda b, n, hp, g=g: (b * tiles + n, 2 * 6 + 2 * g + hp))

    def prev(kind, g):
        per_tile = A_TILE // (WIN * DSW_GROUPS[g][1])
        return lambda b, n, hp: (jnp.maximum((b * tiles + n) * per_tile - 1, 0), kind * 6 + 2 * g + hp)

    in_specs = (
        [pl.BlockSpec((A_TILE, LANES), qcol(g)) for g in range(N_GROUPS)]
        + [pl.BlockSpec((A_TILE, LANES), kcol(g)) for g in range(N_GROUPS)]
        + [pl.BlockSpec((A_TILE, LANES), vcol(g)) for g in range(N_GROUPS)]
        + [pl.BlockSpec((WIN * DSW_GROUPS[g][1], LANES), prev(1, g)) for g in range(N_GROUPS)]
        + [pl.BlockSpec((WIN * DSW_GROUPS[g][1], LANES), prev(2, g)) for g in range(N_GROUPS)]
        + [pl.BlockSpec((N_GROUPS, None, 2, 2 * WIN, 2 * WIN), lambda b, n, hp: (0, hp, 0, 0, 0))]
    )
    return pl.pallas_call(
        _mixer_a_kernel,
        grid=(batch, tiles, 2),
        in_specs=in_specs,
        out_specs=pl.BlockSpec((A_TILE, LANES), lambda b, n, hp: (b * tiles + n, hp)),
        out_shape=jax.ShapeDtypeStruct((t, OUT_A), BF16),
        scratch_shapes=[pltpu.VMEM((N_GROUPS, A_TILE, LANES), F32)] * 3 + [pltpu.VMEM((5, A_TILE, LANES), F32)],
        compiler_params=pltpu.CompilerParams(
            dimension_semantics=("arbitrary", "arbitrary", "arbitrary"), vmem_limit_bytes=VMEM_LIMIT),
        name="mixer_a",
    )(*([pa] * 15), bias)


def _suffix_sum_matrix():
    j = np.arange(B_BLK)[:, None]
    s = np.arange(B_BLK)[None, :]
    later = (j > s).astype(np.float32)
    ones = np.ones((B_BLK, B_BLK), np.float32)
    zero = np.zeros((B_BLK, B_BLK), np.float32)
    return np.block([[later, zero, ones, zero], [zero, later, zero, ones]])


def _mixer_b_kernel(q_ref, k_ref, v_ref, tri_ref, o_ref, acc_s, run_s):
    step = pl.program_id(1)
    tri = tri_ref[...]
    t_loc = lax.broadcasted_iota(jnp.int32, (B_TQ, B_CAT), 0)
    s_loc = jnp.bitwise_and(lax.broadcasted_iota(jnp.int32, (B_TQ, B_CAT), 1), B_BLK - 1)
    per_q = B_TQ // B_BLK
    lane = lax.broadcasted_iota(jnp.int32, (B_BLK, W_B), 1)
    head_lanes = [(lane >= h * HEAD_DIM) & (lane < (h + 1) * HEAD_DIM) for h in range(SB_HEADS)]

    def expand(ref, j):
        blk = ref[pl.ds(pl.multiple_of(j * B_BLK, B_BLK), B_BLK), :]
        return jnp.concatenate([jnp.where(keep, blk, jnp.zeros_like(blk)) for keep in head_lanes], axis=0)

    def block(s, kv, diag_offset, first):
        zn = _dot_nt(q_ref[s * B_TQ:(s + 1) * B_TQ, :], kv[0])
        soft = jnp.log2(1.0 + jnp.exp2(-jnp.abs(zn)))
        log_stay = jnp.minimum(zn, 0.0) - soft
        if diag_offset is not None:
            causal = s_loc + diag_offset < t_loc
            log_stay = jnp.where(causal, log_stay, 0.0)
        ls16 = log_stay.astype(BF16)
        half = SB_HEADS // 2 * B_BLK
        sums = [_dot(ls16[:, p * half:(p + 1) * half], tri) for p in range(2)]
        later = jnp.concatenate([x[:, 0:half] for x in sums], axis=1)
        whole = jnp.concatenate([x[:, half:2 * half] for x in sums], axis=1)
        expo = (log_stay - zn) + later
        if not first:
            run = run_s[s]
            expo = expo + run
            whole = whole + run
        p = jnp.exp2(expo)
        if diag_offset is not None:
            p = jnp.where(causal, p, 0.0)
        run_s[s] = whole
        pv = _dot(p.astype(BF16), kv[1])
        if first:
            acc_s[s] = pv
        else:
            acc_s[s] += pv

    def straight_line(streams_with_trip):
        base = step * (B_STREAMS * per_q)
        cache = {}

        def kv(rel):
            if rel not in cache:
                cache[rel] = (expand(k_ref, base + rel), expand(v_ref, base + rel))
            return cache[rel]

        for s in range(B_STREAMS):
            for d in range(per_q - 1, -1, -1):
                block(s, kv(s * per_q + d), d * B_BLK, first=(d == per_q - 1))
            if s in streams_with_trip:
                for d in range(per_q):
                    block(s, kv(s * per_q - 1 - d), None, first=False)

    def trip(s, jj):
        i = step * B_STREAMS + s
        for d in range(per_q):
            j = (i - jj) * per_q - 1 - d
            block(s, (expand(k_ref, j), expand(v_ref, j)), None, first=False)

    @pl.when(step > 0)
    def _():
        straight_line(range(B_STREAMS))

    @pl.when(step == 0)
    def _():
        straight_line(range(1, B_STREAMS))

    def alive(s):
        run = run_s[s]
        top = run[:, 0:B_BLK]
        for h in range(1, SB_HEADS):
            top = jnp.maximum(top, run[:, h * B_BLK:(h + 1) * B_BLK])
        return jnp.max(top) > RUN_FLOOR

    live0 = [alive(s) for s in range(B_STREAMS)]
    for s in range(B_STREAMS):
        i = step * B_STREAMS + s

        def cond(c, i=i):
            jj, live = c
            return jnp.logical_and(jj < i, live)

        def body(c, s=s):
            jj, _ = c
            trip(s, jj)
            return jj + 1, alive(s)

        lax.while_loop(cond, body, (jnp.int32(1), live0[s]))

    o_ref[...] = acc_s[...].reshape(B_STREAMS * B_TQ, W_B).astype(o_ref.dtype)


def _mixer_b(qb, kb, vb, tri, batch, seq):
    t = qb.shape[0]
    rows = B_STREAMS * B_TQ
    whole_seq = pl.BlockSpec((None, seq, W_B), lambda b, i: (b, 0, 0))
    out = pl.pallas_call(
        _mixer_b_kernel,
        grid=(batch, seq // rows),
        in_specs=[
            pl.BlockSpec((None, rows, W_B), lambda b, i: (b, i, 0)),
            whole_seq,
            whole_seq,
            pl.BlockSpec((2 * B_BLK, 4 * B_BLK), lambda b, i: (0, 0)),
        ],
        out_specs=pl.BlockSpec((None, rows, W_B), lambda b, i: (b, i, 0)),
        out_shape=jax.ShapeDtypeStruct((batch, seq, W_B), BF16),
        scratch_shapes=[pltpu.VMEM((B_STREAMS, B_TQ, W_B), F32), pltpu.VMEM((B_STREAMS, B_TQ, B_CAT), F32)],
        compiler_params=pltpu.CompilerParams(
            dimension_semantics=("arbitrary", "arbitrary"), vmem_limit_bytes=VMEM_LIMIT),
        name="mixer_b",
    )(qb.reshape(batch, seq, W_B), kb.reshape(batch, seq, W_B), vb.reshape(batch, seq, W_B), tri)
    return out.reshape(t, W_B)


FFN_TM = 512
FFN_TF = 256
HALO = 16


def _merge_ffn_kernel(x_ref, oa_ref, ob_ref, pg_ref, bg_ref, wbr_ref, wo_ref, g_ref, wup_ref, cw_ref, cb_ref,
                      wd_ref, gf_ref, o_ref, x1_s, h_s, up_s, val_s, acc_s, *, tiles_per_seq, final_norm):
    i = pl.program_id(0)

    gates = jax.nn.sigmoid(pg_ref[...].astype(F32) + bg_ref[...])
    ya = _dot(oa_ref[...], wbr_ref[0:OUT_A, :])
    yb = _dot(ob_ref[...], wbr_ref[OUT_A:OUT_A + W_B, :])
    merged = gates[:, 0:D_MODEL] * ya + gates[:, D_MODEL:2 * D_MODEL] * yb
    x1 = x_ref[...] + _dot(merged.astype(BF16), wo_ref[...])
    x1_s[...] = x1

    @pl.when(i == 0)
    def _():
        h_s[FFN_TM:FFN_TM + HALO, :] = jnp.zeros((HALO, D_MODEL), BF16)

    h_s[0:HALO, :] = h_s[FFN_TM:FFN_TM + HALO, :]
    h_s[HALO:HALO + FFN_TM, :] = _rms(x1, g_ref[...]).astype(BF16)
    seq_start = (i % tiles_per_seq) == 0
    n_chunks = D_FF // FFN_TF

    def project(c):
        slot = c % 2
        up = _dot(h_s[...], wup_ref[:, c * FFN_TF:(c + 1) * FFN_TF])
        for ls in range(FFN_TF // LANES):
            lanes = slice(ls * LANES, (ls + 1) * LANES)
            up_s[slot, ls, pl.ds(0, HALO, stride=2), :] = jnp.where(seq_start, 0.0, up[0:HALO, lanes])
            up_s[slot, ls, pl.ds(2 * HALO, FFN_TM, stride=2), :] = up[HALO:HALO + FFN_TM, lanes]
        val_s[slot] = _dot(h_s[HALO:HALO + FFN_TM, :], wup_ref[:, D_FF + c * FFN_TF:D_FF + (c + 1) * FFN_TF])

    project(0)
    for c in range(n_chunks):
        if c + 1 < n_chunks:
            project(c + 1)
        slot = c % 2
        cols = slice(c * FFN_TF, (c + 1) * FFN_TF)
        parts = []
        for ls in range(FFN_TF // LANES):
            lanes = slice(c * FFN_TF + ls * LANES, c * FFN_TF + (ls + 1) * LANES)
            a = cb_ref[:, lanes]
            for tap in range(CONV_WIDTH):
                lag = CONV_WIDTH - 1 - tap
                a = a + up_s[slot, ls, pl.ds(2 * (HALO - lag), FFN_TM, stride=2), :] * cw_ref[tap:tap + 1, lanes]
            parts.append(a)
        a = jnp.concatenate(parts, axis=1)
        gelu = 0.5 * a * (1.0 + lax.erf(a * (2.0 ** -0.5)))
        down = _dot((gelu * val_s[slot]).astype(BF16), wd_ref[cols, :])
        if c == 0:
            acc_s[...] = down
        else:
            acc_s[...] += down

    y = x1_s[...] + acc_s[...]
    if final_norm:
        y = _rms(y, gf_ref[...])
    o_ref[...] = y


def _merge_ffn(x, oa, ob, pg, bg, wbr, wo, g, w_up, cw, cb, w_down, gf, layer, seq, final_norm):
    t = x.shape[0]
    kern = functools.partial(_merge_ffn_kernel, tiles_per_seq=seq // FFN_TM, final_norm=final_norm)
    row = lambda w: pl.BlockSpec((FFN_TM, w), lambda i: (i, 0))
    resident = lambda r, c: _layer_spec(layer, r, c, pipeline_mode=pl.Buffered(1))
    return pl.pallas_call(
        kern,
        grid=(t // FFN_TM,),
        in_specs=[
            row(D_MODEL), row(OUT_A), row(W_B), row(2 * D_MODEL),
            resident(1, 2 * D_MODEL),
            resident(OUT_A + W_B, D_MODEL),
            resident(D_MODEL, D_MODEL),
            resident(1, D_MODEL),
            resident(D_MODEL, 2 * D_FF),
            resident(CONV_WIDTH, D_FF),
            resident(1, D_FF),
            resident(D_FF, D_MODEL),
            pl.BlockSpec((1, D_MODEL), lambda i: (0, 0)),
        ],
        out_specs=row(D_MODEL),
        out_shape=jax.ShapeDtypeStruct((t, D_MODEL), F32),
        scratch_shapes=[
            pltpu.VMEM((FFN_TM, D_MODEL), F32),
            pltpu.VMEM((HALO + FFN_TM, D_MODEL), BF16),
            pltpu.VMEM((2, FFN_TF // LANES, 2 * (HALO + FFN_TM), LANES), F32),
            pltpu.VMEM((2, FFN_TM, FFN_TF), F32),
            pltpu.VMEM((FFN_TM, D_MODEL), F32),
        ],
        compiler_params=pltpu.CompilerParams(dimension_semantics=("arbitrary",), vmem_limit_bytes=VMEM_LIMIT),
        name="merge_ffn",
    )(x, oa, ob, pg, bg.reshape(-1, 1, 2 * D_MODEL), wbr, wo, g.reshape(-1, 1, D_MODEL), w_up, cw,
      cb.reshape(-1, 1, D_FF), w_down, gf.reshape(1, D_MODEL))


def kernel(x, norm1, w_in, b_gate, w_br, w_o, norm2, w_up, conv_w, conv_b, w_down, norm_f):
    batch, seq, d = x.shape
    depth = norm1.shape[0]
    assert d == D_MODEL and seq % A_TILE == 0 and seq % FFN_TM == 0
    t = batch * seq
    bias = jnp.asarray(_alibi_bias())
    tri = jnp.asarray(_suffix_sum_matrix(), dtype=BF16)
    xf = x.reshape(t, d)
    w_in, w_br, w_o, w_up, w_down = (w.astype(BF16) for w in (w_in, w_br, w_o, w_up, w_down))
    for l in range(depth):
        pa, qb, kb, vb, pg = _in_proj(xf, norm1, w_in, l)
        oa = _mixer_a(pa, bias, batch, seq)
        ob = _mixer_b(qb, kb, vb, tri, batch, seq)
        xf = _merge_ffn(xf, oa, ob, pg, b_gate, w_br, w_o, norm2, w_up, conv_w, conv_b, w_down, norm_f, l, seq,
                        final_norm=(l == depth - 1))
    return xf.reshape(batch, seq, d)
```

```python
import functools

import numpy as np
import jax
import jax.numpy as jnp
from jax import lax
from jax.experimental import pallas as pl
from jax.experimental.pallas import tpu as pltpu

D_MODEL = 1024
HEAD_DIM = 64
DSW_GROUPS = ((128, 1), (512, 4), (2048, 16))
N_GROUPS = len(DSW_GROUPS)
HEADS_PER_GROUP = 4
DSW_HEADS = HEADS_PER_GROUP * N_GROUPS
SB_HEADS = 4
W_A = DSW_HEADS * HEAD_DIM
W_B = SB_HEADS * HEAD_DIM
OUT_A = HEADS_PER_GROUP * HEAD_DIM
N_IN = 3 * W_A + 3 * W_B + 2 * D_MODEL
D_FF = 2816
CONV_WIDTH = 3
RMS_EPS = 1e-6
QK_SCALE = HEAD_DIM ** -0.5

LANES = 128
WIN = 128
MASKED = -1e30

VMEM_LIMIT = 56 * 1024 * 1024

F32 = jnp.float32
BF16 = jnp.bfloat16


def _dot(a, b):
    return jnp.dot(a, b, preferred_element_type=F32)


def _dot_nt(a, b):
    return lax.dot_general(a, b, (((1,), (1,)), ((), ())), preferred_element_type=F32)


def _rms(x, g):
    ms = jnp.mean(x * x, axis=-1, keepdims=True)
    return x * lax.rsqrt(ms + RMS_EPS) * g


IN_TM = 512
IN_CHUNK = 256
B_BLK = 128
B_TQ = 256
B_STREAMS = 4
B_CAT = SB_HEADS * B_BLK
LOG2E = 1.4426950408889634
RUN_FLOOR = -160.0


def _in_proj_kernel(x0_ref, xn_ref, g_ref, w_ref, pa_ref, qb_ref, kb_ref, vb_ref, pg_ref, h_s):
    @pl.when(pl.program_id(0) == 0)
    def _():
        h_s[...] = _rms(x0_ref[...], g_ref[...]).astype(BF16)

    h = h_s[...]
    h_next = _rms(xn_ref[...], g_ref[...]).astype(BF16)

    def proj(base):
        return _dot(h, w_ref[:, base:base + IN_CHUNK])

    for c in range(0, 3 * W_A, IN_CHUNK):
        pa_ref[:, c:c + IN_CHUNK] = proj(c)
    base_b = 3 * W_A
    qb_ref[...] = (proj(base_b) * (-QK_SCALE * LOG2E)).astype(BF16)
    kb_ref[...] = proj(base_b + W_B).astype(BF16)
    vb_ref[...] = proj(base_b + 2 * W_B).astype(BF16)
    base_g = 3 * W_A + 3 * W_B
    for c in range(0, 2 * D_MODEL, IN_CHUNK):
        pg_ref[:, c:c + IN_CHUNK] = proj(base_g + c).astype(BF16)
    h_s[...] = h_next


def _layer_spec(layer, rows, cols, **kw):
    return pl.BlockSpec((None, rows, cols), lambda i: (layer, 0, 0), **kw)


def _in_proj(x, g, w, layer):
    t = x.shape[0]
    last = t // IN_TM - 1
    return pl.pallas_call(
        _in_proj_kernel,
        grid=(t // IN_TM,),
        in_specs=[
            pl.BlockSpec((IN_TM, D_MODEL), lambda i: (0, 0), pipeline_mode=pl.Buffered(1)),
            pl.BlockSpec((IN_TM, D_MODEL), lambda i: (jnp.minimum(i + 1, last), 0)),
            _layer_spec(layer, 1, D_MODEL),
            _layer_spec(layer, D_MODEL, N_IN),
        ],
        out_specs=[
            pl.BlockSpec((IN_TM, 3 * W_A), lambda i: (i, 0)),
            pl.BlockSpec((IN_TM, W_B), lambda i: (i, 0)),
            pl.BlockSpec((IN_TM, W_B), lambda i: (i, 0)),
            pl.BlockSpec((IN_TM, W_B), lambda i: (i, 0)),
            pl.BlockSpec((IN_TM, 2 * D_MODEL), lambda i: (i, 0)),
        ],
        out_shape=[
            jax.ShapeDtypeStruct((t, 3 * W_A), F32),
            jax.ShapeDtypeStruct((t, W_B), BF16),
            jax.ShapeDtypeStruct((t, W_B), BF16),
            jax.ShapeDtypeStruct((t, W_B), BF16),
            jax.ShapeDtypeStruct((t, 2 * D_MODEL), BF16),
        ],
        scratch_shapes=[pltpu.VMEM((IN_TM, D_MODEL), BF16)],
        compiler_params=pltpu.CompilerParams(dimension_semantics=("arbitrary",), vmem_limit_bytes=VMEM_LIMIT),
        name="in_proj",
    )(x, x, g.reshape(-1, 1, D_MODEL), w)


A_TILE = WIN * max(d for _, d in DSW_GROUPS)


def _alibi_bias():
    slopes = 2.0 ** (-8.0 * np.arange(1, DSW_HEADS + 1) / DSW_HEADS)
    qi = np.arange(WIN)[:, None]
    kj = np.arange(2 * WIN)[None, :]
    delta = qi + WIN - kj
    valid = (delta >= 0) & (delta <= WIN)
    out = np.empty((N_GROUPS, 2, 2, 2 * WIN, 2 * WIN), np.float32)
    for g, (_, dil) in enumerate(DSW_GROUPS):
        for h in range(HEADS_PER_GROUP):
            b = -slopes[g * HEADS_PER_GROUP + h] * LOG2E * (delta * dil).astype(np.float32)
            rows = slice((h % 2) * WIN, (h % 2 + 1) * WIN)
            out[g, h // 2, 0, rows] = np.where(valid, b, MASKED)
            out[g, h // 2, 1, rows] = np.where(valid & (kj >= WIN), b, MASKED)
    return out


A_MAX_STRIDE = 4
A_UNROLL = {1: 15, 4: 12, 16: 16}


def _mixer_a_kernel(*refs):
    q_refs = refs[0:3]
    kc_refs = refs[3:6]
    vc_refs = refs[6:9]
    kp_refs = refs[9:12]
    vp_refs = refs[12:15]
    bias_ref = refs[15]
    o_ref = refs[16]
    acc_s, m_s, l_s, stage_s = refs[17:21]

    first = (pl.program_id(1) == 0).astype(jnp.int32)
    lane = lax.broadcasted_iota(jnp.int32, (WIN, LANES), 1)
    head0 = lane < HEAD_DIM
    ones = jnp.ones((2 * WIN, LANES), BF16)
    n_units = A_TILE // WIN

    for g, (_, dil) in enumerate(DSW_GROUPS):
        q_ref, kc_ref, vc_ref = q_refs[g], kc_refs[g], vc_refs[g]

        staged = {}
        span = A_TILE // A_MAX_STRIDE
        if dil > A_MAX_STRIDE:
            for slot, ref in enumerate((q_ref, kc_ref, vc_ref, kp_refs[g], vp_refs[g])):
                for r in range(A_MAX_STRIDE):
                    stage_s[slot, r * span:(r + 1) * span, :] = ref[pl.ds(r, span, stride=A_MAX_STRIDE), :]
                staged[id(ref)] = slot

        def rows(ref, start, dil=dil, staged=staged, span=span):
            if dil == 1:
                return ref[pl.ds(start, WIN), :]
            if id(ref) in staged:
                first_row = (start % A_MAX_STRIDE) * span + start // A_MAX_STRIDE
                return stage_s[staged[id(ref)], pl.ds(first_row, WIN, stride=dil // A_MAX_STRIDE), :]
            return ref[pl.ds(start, WIN, stride=dil), :]

        def unit(kprev_ref, vprev_ref, prev_off, off, sel, g=g, dil=dil, q_ref=q_ref, kc_ref=kc_ref,
                 vc_ref=vc_ref, rows=rows):
            q = rows(q_ref, off) * (QK_SCALE * LOG2E)
            q2 = jnp.concatenate([jnp.where(head0, q, 0.0), jnp.where(head0, 0.0, q)], axis=0).astype(BF16)
            k = jnp.concatenate([rows(kprev_ref, prev_off), rows(kc_ref, off)], axis=0).astype(BF16)
            v = jnp.concatenate([rows(vprev_ref, prev_off), rows(vc_ref, off)], axis=0).astype(BF16)
            s = _dot_nt(q2, k) + bias_ref[g, sel]
            m = jnp.max(s, axis=-1, keepdims=True)
            p = jnp.exp2(s - m).astype(BF16)
            pvl = _dot(p, jnp.concatenate([v, ones], axis=1))
            dst = pl.ds(off, WIN) if dil == 1 else pl.ds(off, WIN, stride=dil)
            acc_s[g, dst, :] = jnp.where(head0, pvl[0:WIN, 0:LANES], pvl[WIN:2 * WIN, 0:LANES])
            l_s[g, dst, :] = jnp.where(head0, pvl[0:WIN, LANES:2 * LANES], pvl[WIN:2 * WIN, LANES:2 * LANES])
            m_s[g, dst, :] = jnp.where(head0, m[0:WIN], m[WIN:2 * WIN])

        def head_unit(r, unit=unit, kp_ref=kp_refs[g], vp_ref=vp_refs[g]):
            unit(kp_ref, vp_ref, r, r, first)

        def tail_unit(idx, unit=unit, kc_ref=kc_ref, vc_ref=vc_ref, dil=dil):
            shift = dil.bit_length() - 1
            u = idx >> shift if isinstance(idx, int) else lax.shift_right_logical(idx, shift)
            off = u * (WIN * dil) + (idx & (dil - 1))
            unit(kc_ref, vc_ref, off - WIN * dil, off, 0)

        unroll = A_UNROLL[dil]
        for fn, lo, hi in ((head_unit, 0, dil), (tail_unit, dil, n_units)):
            count = hi - lo
            if count == 0:
                continue
            if count <= unroll:
                for idx in range(lo, hi):
                    fn(idx)
                continue
            assert count % unroll == 0

            def body(t, c, fn=fn, lo=lo, unroll=unroll):
                for uu in range(unroll):
                    fn(lo + t * unroll + uu)
                return c

            lax.fori_loop(0, count // unroll, body, 0)

    m_all = jnp.maximum(jnp.maximum(m_s[0], m_s[1]), m_s[2])
    num = jnp.zeros((A_TILE, LANES), F32)
    den = jnp.zeros((A_TILE, LANES), F32)
    for g in range(N_GROUPS):
        w = jnp.exp2(m_s[g] - m_all)
        num = num + w * acc_s[g]
        den = den + w * l_s[g]
    o_ref[...] = (num / den).astype(o_ref.dtype)


def _mixer_a(pa, bias, batch, seq):
    t = pa.shape[0]
    tiles = seq // A_TILE
    qcol = lambda g: (lambda b, n, hp, g=g: (b * tiles + n, 0 * 6 + 2 * g + hp))
    kcol = lambda g: (lambda b, n, hp, g=g: (b * tiles + n, 1 * 6 + 2 * g + hp))
    vcol = lambda g: (lambda b, n, hp, g=g: (b * tiles + n, 2 * 6 + 2 * g + hp))

    def prev(kind, g):
        per_tile = A_TILE // (WIN * DSW_GROUPS[g][1])
        return lambda b, n, hp: (jnp.maximum((b * tiles + n) * per_tile - 1, 0), kind * 6 + 2 * g + hp)

    in_specs = (
        [pl.BlockSpec((A_TILE, LANES), qcol(g)) for g in range(N_GROUPS)]
        + [pl.BlockSpec((A_TILE, LANES), kcol(g)) for g in range(N_GROUPS)]
        + [pl.BlockSpec((A_TILE, LANES), vcol(g)) for g in range(N_GROUPS)]
        + [pl.BlockSpec((WIN * DSW_GROUPS[g][1], LANES), prev(1, g)) for g in range(N_GROUPS)]
        + [pl.BlockSpec((WIN * DSW_GROUPS[g][1], LANES), prev(2, g)) for g in range(N_GROUPS)]
        + [pl.BlockSpec((N_GROUPS, None, 2, 2 * WIN, 2 * WIN), lambda b, n, hp: (0, hp, 0, 0, 0))]
    )
    return pl.pallas_call(
        _mixer_a_kernel,
        grid=(batch, tiles, 2),
        in_specs=in_specs,
        out_specs=pl.BlockSpec((A_TILE, LANES), lambda b, n, hp: (b * tiles + n, hp)),
        out_shape=jax.ShapeDtypeStruct((t, OUT_A), BF16),
        scratch_shapes=[pltpu.VMEM((N_GROUPS, A_TILE, LANES), F32)] * 3 + [pltpu.VMEM((5, A_TILE, LANES), F32)],
        compiler_params=pltpu.CompilerParams(
            dimension_semantics=("arbitrary", "arbitrary", "arbitrary"), vmem_limit_bytes=VMEM_LIMIT),
        name="mixer_a",
    )(*([pa] * 15), bias)


def _suffix_sum_matrix():
    j = np.arange(B_BLK)[:, None]
    s = np.arange(B_BLK)[None, :]
    later = (j > s).astype(np.float32)
    ones = np.ones((B_BLK, B_BLK), np.float32)
    zero = np.zeros((B_BLK, B_BLK), np.float32)
    return np.block([[later, zero, ones, zero], [zero, later, zero, ones]])


def _mixer_b_kernel(q_ref, k_ref, v_ref, tri_ref, o_ref, acc_s, run_s):
    step = pl.program_id(1)
    tri = tri_ref[...]
    t_loc = lax.broadcasted_iota(jnp.int32, (B_TQ, B_CAT), 0)
    s_loc = jnp.bitwise_and(lax.broadcasted_iota(jnp.int32, (B_TQ, B_CAT), 1), B_BLK - 1)
    per_q = B_TQ // B_BLK
    lane = lax.broadcasted_iota(jnp.int32, (B_BLK, W_B), 1)
    head_lanes = [(lane >= h * HEAD_DIM) & (lane < (h + 1) * HEAD_DIM) for h in range(SB_HEADS)]

    def expand(ref, j):
        blk = ref[pl.ds(pl.multiple_of(j * B_BLK, B_BLK), B_BLK), :]
        return jnp.concatenate([jnp.where(keep, blk, jnp.zeros_like(blk)) for keep in head_lanes], axis=0)

    def block(s, kv, diag_offset, first):
        zn = _dot_nt(q_ref[s * B_TQ:(s + 1) * B_TQ, :], kv[0])
        soft = jnp.log2(1.0 + jnp.exp2(-jnp.abs(zn)))
        log_stay = jnp.minimum(zn, 0.0) - soft
        if diag_offset is not None:
            causal = s_loc + diag_offset < t_loc
            log_stay = jnp.where(causal, log_stay, 0.0)
        ls16 = log_stay.astype(BF16)
        half = SB_HEADS // 2 * B_BLK
        sums = [_dot(ls16[:, p * half:(p + 1) * half], tri) for p in range(2)]
        later = jnp.concatenate([x[:, 0:half] for x in sums], axis=1)
        whole = jnp.concatenate([x[:, half:2 * half] for x in sums], axis=1)
        expo = (log_stay - zn) + later
        if not first:
            run = run_s[s]
            expo = expo + run
            whole = whole + run
        p = jnp.exp2(expo)
        if diag_offset is not None:
            p = jnp.where(causal, p, 0.0)
        run_s[s] = whole
        pv = _dot(p.astype(BF16), kv[1])
        if first:
            acc_s[s] = pv
        else:
            acc_s[s] += pv

    def straight_line(streams_with_trip):
        base = step * (B_STREAMS * per_q)
        cache = {}

        def kv(rel):
            if rel not in cache:
                cache[rel] = (expand(k_ref, base + rel), expand(v_ref, base + rel))
            return cache[rel]

        for s in range(B_STREAMS):
            for d in range(per_q - 1, -1, -1):
                block(s, kv(s * per_q + d), d * B_BLK, first=(d == per_q - 1))
            if s in streams_with_trip:
                for d in range(per_q):
                    block(s, kv(s * per_q - 1 - d), None, first=False)

    def trip(s, jj):
        i = step * B_STREAMS + s
        for d in range(per_q):
            j = (i - jj) * per_q - 1 - d
            block(s, (expand(k_ref, j), expand(v_ref, j)), None, first=False)

    @pl.when(step > 0)
    def _():
        straight_line(range(B_STREAMS))

    @pl.when(step == 0)
    def _():
        straight_line(range(1, B_STREAMS))

    def alive(s):
        run = run_s[s]
        top = run[:, 0:B_BLK]
        for h in range(1, SB_HEADS):
            top = jnp.maximum(top, run[:, h * B_BLK:(h + 1) * B_BLK])
        return jnp.max(top) > RUN_FLOOR

    live0 = [alive(s) for s in range(B_STREAMS)]
    for s in range(B_STREAMS):
        i = step * B_STREAMS + s

        def cond(c, i=i):
            jj, live = c
            return jnp.logical_and(jj < i, live)

        def body(c, s=s):
            jj, _ = c
            trip(s, jj)
            return jj + 1, alive(s)

        lax.while_loop(cond, body, (jnp.int32(1), live0[s]))

    o_ref[...] = acc_s[...].reshape(B_STREAMS * B_TQ, W_B).astype(o_ref.dtype)


def _mixer_b(qb, kb, vb, tri, batch, seq):
    t = qb.shape[0]
    rows = B_STREAMS * B_TQ
    whole_seq = pl.BlockSpec((None, seq, W_B), lambda b, i: (b, 0, 0))
    out = pl.pallas_call(
        _mixer_b_kernel,
        grid=(batch, seq // rows),
        in_specs=[
            pl.BlockSpec((None, rows, W_B), lambda b, i: (b, i, 0)),
            whole_seq,
            whole_seq,
            pl.BlockSpec((2 * B_BLK, 4 * B_BLK), lambda b, i: (0, 0)),
        ],
        out_specs=pl.BlockSpec((None, rows, W_B), lambda b, i: (b, i, 0)),
        out_shape=jax.ShapeDtypeStruct((batch, seq, W_B), BF16),
        scratch_shapes=[pltpu.VMEM((B_STREAMS, B_TQ, W_B), F32), pltpu.VMEM((B_STREAMS, B_TQ, B_CAT), F32)],
        compiler_params=pltpu.CompilerParams(
            dimension_semantics=("arbitrary", "arbitrary"), vmem_limit_bytes=VMEM_LIMIT),
        name="mixer_b",
    )(qb.reshape(batch, seq, W_B), kb.reshape(batch, seq, W_B), vb.reshape(batch, seq, W_B), tri)
    return out.reshape(t, W_B)


FFN_TM = 512
FFN_TF = 256
HALO = 16


def _merge_ffn_kernel(x_ref, oa_ref, ob_ref, pg_ref, bg_ref, wbr_ref, wo_ref, g_ref, wup_ref, cw_ref, cb_ref,
                      wd_ref, gf_ref, o_ref, x1_s, h_s, up_s, val_s, acc_s, *, tiles_per_seq, final_norm):
    i = pl.program_id(0)

    gates = jax.nn.sigmoid(pg_ref[...].astype(F32) + bg_ref[...])
    ya = _dot(oa_ref[...], wbr_ref[0:OUT_A, :])
    yb = _dot(ob_ref[...], wbr_ref[OUT_A:OUT_A + W_B, :])
    merged = gates[:, 0:D_MODEL] * ya + gates[:, D_MODEL:2 * D_MODEL] * yb
    x1 = x_ref[...] + _dot(merged.astype(BF16), wo_ref[...])
    x1_s[...] = x1

    @pl.when(i == 0)
    def _():
        h_s[FFN_TM:FFN_TM + HALO, :] = jnp.zeros((HALO, D_MODEL), BF16)

    h_s[0:HALO, :] = h_s[FFN_TM:FFN_TM + HALO, :]
    h_s[HALO:HALO + FFN_TM, :] = _rms(x1, g_ref[...]).astype(BF16)
    seq_start = (i % tiles_per_seq) == 0
    n_chunks = D_FF // FFN_TF

    def project(c):
        slot = c % 2
        up = _dot(h_s[...], wup_ref[:, c * FFN_TF:(c + 1) * FFN_TF])
        for ls in range(FFN_TF // LANES):
            lanes = slice(ls * LANES, (ls + 1) * LANES)
            up_s[slot, ls, pl.ds(0, HALO, stride=2), :] = jnp.where(seq_start, 0.0, up[0:HALO, lanes])
            up_s[slot, ls, pl.ds(2 * HALO, FFN_TM, stride=2), :] = up[HALO:HALO + FFN_TM, lanes]
        val_s[slot] = _dot(h_s[HALO:HALO + FFN_TM, :], wup_ref[:, D_FF + c * FFN_TF:D_FF + (c + 1) * FFN_TF])

    project(0)
    for c in range(n_chunks):
        if c + 1 < n_chunks:
            project(c + 1)
        slot = c % 2
        cols = slice(c * FFN_TF, (c + 1) * FFN_TF)
        parts = []
        for ls in range(FFN_TF // LANES):
            lanes = slice(c * FFN_TF + ls * LANES, c * FFN_TF + (ls + 1) * LANES)
            a = cb_ref[:, lanes]
            for tap in range(CONV_WIDTH):
                lag = CONV_WIDTH - 1 - tap
                a = a + up_s[slot, ls, pl.ds(2 * (HALO - lag), FFN_TM, stride=2), :] * cw_ref[tap:tap + 1, lanes]
            parts.append(a)
        a = jnp.concatenate(parts, axis=1)
        gelu = 0.5 * a * (1.0 + lax.erf(a * (2.0 ** -0.5)))
        down = _dot((gelu * val_s[slot]).astype(BF16), wd_ref[cols, :])
        if c == 0:
            acc_s[...] = down
        else:
            acc_s[...] += down

    y = x1_s[...] + acc_s[...]
    if final_norm:
        y = _rms(y, gf_ref[...])
    o_ref[...] = y


def _merge_ffn(x, oa, ob, pg, bg, wbr, wo, g, w_up, cw, cb, w_down, gf, layer, seq, final_norm):
    t = x.shape[0]
    kern = functools.partial(_merge_ffn_kernel, tiles_per_seq=seq // FFN_TM, final_norm=final_norm)
    row = lambda w: pl.BlockSpec((FFN_TM, w), lambda i: (i, 0))
    resident = lambda r, c: _layer_spec(layer, r, c, pipeline_mode=pl.Buffered(1))
    return pl.pallas_call(
        kern,
        grid=(t // FFN_TM,),
        in_specs=[
            row(D_MODEL), row(OUT_A), row(W_B), row(2 * D_MODEL),
            resident(1, 2 * D_MODEL),
            resident(OUT_A + W_B, D_MODEL),
            resident(D_MODEL, D_MODEL),
            resident(1, D_MODEL),
            resident(D_MODEL, 2 * D_FF),
            resident(CONV_WIDTH, D_FF),
            resident(1, D_FF),
            resident(D_FF, D_MODEL),
            pl.BlockSpec((1, D_MODEL), lambda i: (0, 0)),
        ],
        out_specs=row(D_MODEL),
        out_shape=jax.ShapeDtypeStruct((t, D_MODEL), F32),
        scratch_shapes=[
            pltpu.VMEM((FFN_TM, D_MODEL), F32),
            pltpu.VMEM((HALO + FFN_TM, D_MODEL), BF16),
            pltpu.VMEM((2, FFN_TF // LANES, 2 * (HALO + FFN_TM), LANES), F32),
            pltpu.VMEM((2, FFN_TM, FFN_TF), F32),
            pltpu.VMEM((FFN_TM, D_MODEL), F32),
        ],
        compiler_params=pltpu.CompilerParams(dimension_semantics=("arbitrary",), vmem_limit_bytes=VMEM_LIMIT),
        name="merge_ffn",
    )(x, oa, ob, pg, bg.reshape(-1, 1, 2 * D_MODEL), wbr, wo, g.reshape(-1, 1, D_MODEL), w_up, cw,
      cb.reshape(-1, 1, D_FF), w_down, gf.reshape(1, D_MODEL))


def kernel(x, norm1, w_in, b_gate, w_br, w_o, norm2, w_up, conv_w, conv_b, w_down, norm_f):
    batch, seq, d = x.shape
    depth = norm1.shape[0]
    assert d == D_MODEL and seq % A_TILE == 0 and seq % FFN_TM == 0
    t = batch * seq
    bias = jnp.asarray(_alibi_bias())
    tri = jnp.asarray(_suffix_sum_matrix(), dtype=BF16)
    xf = x.reshape(t, d)
    w_in, w_br, w_o, w_up, w_down = (w.astype(BF16) for w in (w_in, w_br, w_o, w_up, w_down))
    for l in range(depth):
        pa, qb, kb, vb, pg = _in_proj(xf, norm1, w_in, l)
        oa = _mixer_a(pa, bias, batch, seq)
        ob = _mixer_b(qb, kb, vb, tri, batch, seq)
        xf = _merge_ffn(xf, oa, ob, pg, b_gate, w_br, w_o, norm2, w_up, conv_w, conv_b, w_down, norm_f, l, seq,
                        final_norm=(l == depth - 1))
    return xf.reshape(batch, seq, d)
```

```python
import functools

import numpy as np
import jax
import jax.numpy as jnp
from jax import lax
from jax.experimental import pallas as pl
from jax.experimental.pallas import tpu as pltpu

D_MODEL = 1024
HEAD_DIM = 64
DSW_GROUPS = ((128, 1), (512, 4), (2048, 16))
N_GROUPS = len(DSW_GROUPS)
HEADS_PER_GROUP = 4
DSW_HEADS = HEADS_PER_GROUP * N_GROUPS
SB_HEADS = 4
W_A = DSW_HEADS * HEAD_DIM
W_B = SB_HEADS * HEAD_DIM
OUT_A = HEADS_PER_GROUP * HEAD_DIM
N_IN = 3 * W_A + 3 * W_B + 2 * D_MODEL
D_FF = 2816
CONV_WIDTH = 3
RMS_EPS = 1e-6
QK_SCALE = HEAD_DIM ** -0.5

LANES = 128
WIN = 128
MASKED = -1e30

VMEM_LIMIT = 56 * 1024 * 1024

F32 = jnp.float32
BF16 = jnp.bfloat16


def _dot(a, b):
    return jnp.dot(a, b, preferred_element_type=F32)


def _dot_nt(a, b):
    return lax.dot_general(a, b, (((1,), (1,)), ((), ())), preferred_element_type=F32)


def _rms(x, g):
    ms = jnp.mean(x * x, axis=-1, keepdims=True)
    return x * lax.rsqrt(ms + RMS_EPS) * g


IN_TM = 512
IN_CHUNK = 256
B_BLK = 128
B_TQ = 256
B_STREAMS = 4
B_CAT = SB_HEADS * B_BLK
LOG2E = 1.4426950408889634
RUN_FLOOR = -160.0


def _in_proj_kernel(x0_ref, xn_ref, g_ref, w_ref, pa_ref, qb_ref, kb_ref, vb_ref, pg_ref, h_s):
    @pl.when(pl.program_id(0) == 0)
    def _():
        h_s[...] = _rms(x0_ref[...], g_ref[...]).astype(BF16)

    h = h_s[...]
    h_next = _rms(xn_ref[...], g_ref[...]).astype(BF16)

    def proj(base):
        return _dot(h, w_ref[:, base:base + IN_CHUNK])

    for c in range(0, 3 * W_A, IN_CHUNK):
        pa_ref[:, c:c + IN_CHUNK] = proj(c)
    base_b = 3 * W_A
    qb_ref[...] = (proj(base_b) * (-QK_SCALE * LOG2E)).astype(BF16)
    kb_ref[...] = proj(base_b + W_B).astype(BF16)
    vb_ref[...] = proj(base_b + 2 * W_B).astype(BF16)
    base_g = 3 * W_A + 3 * W_B
    for c in range(0, 2 * D_MODEL, IN_CHUNK):
        pg_ref[:, c:c + IN_CHUNK] = proj(base_g + c).astype(BF16)
    h_s[...] = h_next


def _layer_spec(layer, rows, cols, **kw):
    return pl.BlockSpec((None, rows, cols), lambda i: (layer, 0, 0), **kw)


def _in_proj(x, g, w, layer):
    t = x.shape[0]
    last = t // IN_TM - 1
    return pl.pallas_call(
        _in_proj_kernel,
        grid=(t // IN_TM,),
        in_specs=[
            pl.BlockSpec((IN_TM, D_MODEL), lambda i: (0, 0), pipeline_mode=pl.Buffered(1)),
            pl.BlockSpec((IN_TM, D_MODEL), lambda i: (jnp.minimum(i + 1, last), 0)),
            _layer_spec(layer, 1, D_MODEL),
            _layer_spec(layer, D_MODEL, N_IN),
        ],
        out_specs=[
            pl.BlockSpec((IN_TM, 3 * W_A), lambda i: (i, 0)),
            pl.BlockSpec((IN_TM, W_B), lambda i: (i, 0)),
            pl.BlockSpec((IN_TM, W_B), lambda i: (i, 0)),
            pl.BlockSpec((IN_TM, W_B), lambda i: (i, 0)),
            pl.BlockSpec((IN_TM, 2 * D_MODEL), lambda i: (i, 0)),
        ],
        out_shape=[
            jax.ShapeDtypeStruct((t, 3 * W_A), F32),
            jax.ShapeDtypeStruct((t, W_B), BF16),
            jax.ShapeDtypeStruct((t, W_B), BF16),
            jax.ShapeDtypeStruct((t, W_B), BF16),
            jax.ShapeDtypeStruct((t, 2 * D_MODEL), BF16),
        ],
        scratch_shapes=[pltpu.VMEM((IN_TM, D_MODEL), BF16)],
        compiler_params=pltpu.CompilerParams(dimension_semantics=("arbitrary",), vmem_limit_bytes=VMEM_LIMIT),
        name="in_proj",
    )(x, x, g.reshape(-1, 1, D_MODEL), w)


A_TILE = WIN * max(d for _, d in DSW_GROUPS)


def _alibi_bias():
    slopes = 2.0 ** (-8.0 * np.arange(1, DSW_HEADS + 1) / DSW_HEADS)
    qi = np.arange(WIN)[:, None]
    kj = np.arange(2 * WIN)[None, :]
    delta = qi + WIN - kj
    valid = (delta >= 0) & (delta <= WIN)
    out = np.empty((N_GROUPS, 2, 2, 2 * WIN, 2 * WIN), np.float32)
    for g, (_, dil) in enumerate(DSW_GROUPS):
        for h in range(HEADS_PER_GROUP):
            b = -slopes[g * HEADS_PER_GROUP + h] * LOG2E * (delta * dil).astype(np.float32)
            rows = slice((h % 2) * WIN, (h % 2 + 1) * WIN)
            out[g, h // 2, 0, rows] = np.where(valid, b, MASKED)
            out[g, h // 2, 1, rows] = np.where(valid & (kj >= WIN), b, MASKED)
    return out


A_MAX_STRIDE = 4
A_UNROLL = {1: 15, 4: 12, 16: 16}


def _mixer_a_kernel(*refs):
    q_refs = refs[0:3]
    kc_refs = refs[3:6]
    vc_refs = refs[6:9]
    kp_refs = refs[9:12]
    vp_refs = refs[12:15]
    bias_ref = refs[15]
    o_ref = refs[16]
    acc_s, m_s, l_s, stage_s = refs[17:21]

    first = (pl.program_id(1) == 0).astype(jnp.int32)
    lane = lax.broadcasted_iota(jnp.int32, (WIN, LANES), 1)
    head0 = lane < HEAD_DIM
    ones = jnp.ones((2 * WIN, LANES), BF16)
    n_units = A_TILE // WIN

    for g, (_, dil) in enumerate(DSW_GROUPS):
        q_ref, kc_ref, vc_ref = q_refs[g], kc_refs[g], vc_refs[g]

        staged = {}
        span = A_TILE // A_MAX_STRIDE
        if dil > A_MAX_STRIDE:
            for slot, ref in enumerate((q_ref, kc_ref, vc_ref, kp_refs[g], vp_refs[g])):
                for r in range(A_MAX_STRIDE):
                    stage_s[slot, r * span:(r + 1) * span, :] = ref[pl.ds(r, span, stride=A_MAX_STRIDE), :]
                staged[id(ref)] = slot

        def rows(ref, start, dil=dil, staged=staged, span=span):
            if dil == 1:
                return ref[pl.ds(start, WIN), :]
            if id(ref) in staged:
                first_row = (start % A_MAX_STRIDE) * span + start // A_MAX_STRIDE
                return stage_s[staged[id(ref)], pl.ds(first_row, WIN, stride=dil // A_MAX_STRIDE), :]
            return ref[pl.ds(start, WIN, stride=dil), :]

        def unit(kprev_ref, vprev_ref, prev_off, off, sel, g=g, dil=dil, q_ref=q_ref, kc_ref=kc_ref,
                 vc_ref=vc_ref, rows=rows):
            q = rows(q_ref, off) * (QK_SCALE * LOG2E)
            q2 = jnp.concatenate([jnp.where(head0, q, 0.0), jnp.where(head0, 0.0, q)], axis=0).astype(BF16)
            k = jnp.concatenate([rows(kprev_ref, prev_off), rows(kc_ref, off)], axis=0).astype(BF16)
            v = jnp.concatenate([rows(vprev_ref, prev_off), rows(vc_ref, off)], axis=0).astype(BF16)
            s = _dot_nt(q2, k) + bias_ref[g, sel]
            m = jnp.max(s, axis=-1, keepdims=True)
            p = jnp.exp2(s - m).astype(BF16)
            pvl = _dot(p, jnp.concatenate([v, ones], axis=1))
            dst = pl.ds(off, WIN) if dil == 1 else pl.ds(off, WIN, stride=dil)
            acc_s[g, dst, :] = jnp.where(head0, pvl[0:WIN, 0:LANES], pvl[WIN:2 * WIN, 0:LANES])
            l_s[g, dst, :] = jnp.where(head0, pvl[0:WIN, LANES:2 * LANES], pvl[WIN:2 * WIN, LANES:2 * LANES])
            m_s[g, dst, :] = jnp.where(head0, m[0:WIN], m[WIN:2 * WIN])

        def head_unit(r, unit=unit, kp_ref=kp_refs[g], vp_ref=vp_refs[g]):
            unit(kp_ref, vp_ref, r, r, first)

        def tail_unit(idx, unit=unit, kc_ref=kc_ref, vc_ref=vc_ref, dil=dil):
            shift = dil.bit_length() - 1
            u = idx >> shift if isinstance(idx, int) else lax.shift_right_logical(idx, shift)
            off = u * (WIN * dil) + (idx & (dil - 1))
            unit(kc_ref, vc_ref, off - WIN * dil, off, 0)

        unroll = A_UNROLL[dil]
        for fn, lo, hi in ((head_unit, 0, dil), (tail_unit, dil, n_units)):
            count = hi - lo
            if count == 0:
                continue
            if count <= unroll:
                for idx in range(lo, hi):
                    fn(idx)
                continue
            assert count % unroll == 0

            def body(t, c, fn=fn, lo=lo, unroll=unroll):
                for uu in range(unroll):
                    fn(lo + t * unroll + uu)
                return c

            lax.fori_loop(0, count // unroll, body, 0)

    m_all = jnp.maximum(jnp.maximum(m_s[0], m_s[1]), m_s[2])
    num = jnp.zeros((A_TILE, LANES), F32)
    den = jnp.zeros((A_TILE, LANES), F32)
    for g in range(N_GROUPS):
        w = jnp.exp2(m_s[g] - m_all)
        num = num + w * acc_s[g]
        den = den + w * l_s[g]
    o_ref[...] = (num / den).astype(o_ref.dtype)


def _mixer_a(pa, bias, batch, seq):
    t = pa.shape[0]
    tiles = seq // A_TILE
    qcol = lambda g: (lambda b, n, hp, g=g: (b * tiles + n, 0 * 6 + 2 * g + hp))
    kcol = lambda g: (lambda b, n, hp, g=g: (b * tiles + n, 1 * 6 + 2 * g + hp))
    vcol = lambda g: (lambda b, n, hp, g=g: (b * tiles + n, 2 * 6 + 2 * g + hp))

    def prev(kind, g):
        per_tile = A_TILE // (WIN * DSW_GROUPS[g][1])
        return lambda b, n, hp: (jnp.maximum((b * tiles + n) * per_tile - 1, 0), kind * 6 + 2 * g + hp)

    in_specs = (
        [pl.BlockSpec((A_TILE, LANES), qcol(g)) for g in range(N_GROUPS)]
        + [pl.BlockSpec((A_TILE, LANES), kcol(g)) for g in range(N_GROUPS)]
        + [pl.BlockSpec((A_TILE, LANES), vcol(g)) for g in range(N_GROUPS)]
        + [pl.BlockSpec((WIN * DSW_GROUPS[g][1], LANES), prev(1, g)) for g in range(N_GROUPS)]
        + [pl.BlockSpec((WIN * DSW_GROUPS[g][1], LANES), prev(2, g)) for g in range(N_GROUPS)]
        + [pl.BlockSpec((N_GROUPS, None, 2, 2 * WIN, 2 * WIN), lambda b, n, hp: (0, hp, 0, 0, 0))]
    )
    return pl.pallas_call(
        _mixer_a_kernel,
        grid=(batch, tiles, 2),
        in_specs=in_specs,
        out_specs=pl.BlockSpec((A_TILE, LANES), lambda b, n, hp: (b * tiles + n, hp)),
        out_shape=jax.ShapeDtypeStruct((t, OUT_A), BF16),
        scratch_shapes=[pltpu.VMEM((N_GROUPS, A_TILE, LANES), F32)] * 3 + [pltpu.VMEM((5, A_TILE, LANES), F32)],
        compiler_params=pltpu.CompilerParams(
            dimension_semantics=("arbitrary", "arbitrary", "arbitrary"), vmem_limit_bytes=VMEM_LIMIT),
        name="mixer_a",
    )(*([pa] * 15), bias)


def _suffix_sum_matrix():
    j = np.arange(B_BLK)[:, None]
    s = np.arange(B_BLK)[None, :]
    later = (j > s).astype(np.float32)
    ones = np.ones((B_BLK, B_BLK), np.float32)
    zero = np.zeros((B_BLK, B_BLK), np.float32)
    return np.block([[later, zero, ones, zero], [zero, later, zero, ones]])


def _mixer_b_kernel(q_ref, k_ref, v_ref, tri_ref, o_ref, acc_s, run_s):
    step = pl.program_id(1)
    tri = tri_ref[...]
    per_q = B_TQ // B_BLK
    lane = lax.broadcasted_iota(jnp.int32, (B_BLK, W_B), 1)
    head_lanes = [(lane >= h * HEAD_DIM) & (lane < (h + 1) * HEAD_DIM) for h in range(SB_HEADS)]

    def expand(ref, j):
        blk = ref[pl.ds(pl.multiple_of(j * B_BLK, B_BLK), B_BLK), :]
        return jnp.concatenate([jnp.where(keep, blk, jnp.zeros_like(blk)) for keep in head_lanes], axis=0)

    def block(s, kv, diag_offset, first):
        top = diag_offset or 0
        zn = _dot_nt(q_ref[s * B_TQ + top:(s + 1) * B_TQ, :], kv[0])
        soft = jnp.log2(1.0 + jnp.exp2(-jnp.abs(zn)))
        log_stay = jnp.minimum(zn, 0.0) - soft
        if diag_offset is not None:
            t_loc = lax.broadcasted_iota(jnp.int32, zn.shape, 0) + top
            s_loc = jnp.bitwise_and(lax.broadcasted_iota(jnp.int32, zn.shape, 1), B_BLK - 1)
            causal = s_loc + diag_offset < t_loc
            log_stay = jnp.where(causal, log_stay, 0.0)
        ls16 = log_stay.astype(BF16)
        half = SB_HEADS // 2 * B_BLK
        sums = [_dot(ls16[:, p * half:(p + 1) * half], tri) for p in range(2)]
        later = jnp.concatenate([x[:, 0:half] for x in sums], axis=1)
        whole = jnp.concatenate([x[:, half:2 * half] for x in sums], axis=1)
        expo = (log_stay - zn) + later
        if not first:
            run = run_s[s, top:, :]
            expo = expo + run
            whole = whole + run
        p = jnp.exp2(expo)
        if diag_offset is not None:
            p = jnp.where(causal, p, 0.0)
        run_s[s, top:, :] = whole
        pv = _dot(p.astype(BF16), kv[1])
        if first:
            acc_s[s, top:, :] = pv
            if top:
                run_s[s, 0:top, :] = jnp.zeros((top, B_CAT), F32)
                acc_s[s, 0:top, :] = jnp.zeros((top, W_B), F32)
        else:
            acc_s[s, top:, :] += pv

    def straight_line(streams_with_trip):
        base = step * (B_STREAMS * per_q)
        cache = {}

        def kv(rel):
            if rel not in cache:
                cache[rel] = (expand(k_ref, base + rel), expand(v_ref, base + rel))
            return cache[rel]

        for s in range(B_STREAMS):
            for d in range(per_q - 1, -1, -1):
                block(s, kv(s * per_q + d), d * B_BLK, first=(d == per_q - 1))
            if s in streams_with_trip:
                for d in range(per_q):
                    block(s, kv(s * per_q - 1 - d), None, first=False)

    def trip(s, jj):
        i = step * B_STREAMS + s
        for d in range(per_q):
            j = (i - jj) * per_q - 1 - d
            block(s, (expand(k_ref, j), expand(v_ref, j)), None, first=False)

    @pl.when(step > 0)
    def _():
        straight_line(range(B_STREAMS))

    @pl.when(step == 0)
    def _():
        straight_line(range(1, B_STREAMS))

    def alive(s):
        run = run_s[s]
        top = run[:, 0:B_BLK]
        for h in range(1, SB_HEADS):
            top = jnp.maximum(top, run[:, h * B_BLK:(h + 1) * B_BLK])
        return jnp.max(top) > RUN_FLOOR

    live0 = [alive(s) for s in range(B_STREAMS)]
    for s in range(B_STREAMS):
        i = step * B_STREAMS + s

        def cond(c, i=i):
            jj, live = c
            return jnp.logical_and(jj < i, live)

        def body(c, s=s):
            jj, _ = c
            trip(s, jj)
            return jj + 1, alive(s)

        lax.while_loop(cond, body, (jnp.int32(1), live0[s]))

    o_ref[...] = acc_s[...].reshape(B_STREAMS * B_TQ, W_B).astype(o_ref.dtype)


def _mixer_b(qb, kb, vb, tri, batch, seq):
    t = qb.shape[0]
    rows = B_STREAMS * B_TQ
    whole_seq = pl.BlockSpec((None, seq, W_B), lambda b, i: (b, 0, 0))
    out = pl.pallas_call(
        _mixer_b_kernel,
        grid=(batch, seq // rows),
        in_specs=[
            pl.BlockSpec((None, rows, W_B), lambda b, i: (b, i, 0)),
            whole_seq,
            whole_seq,
            pl.BlockSpec((2 * B_BLK, 4 * B_BLK), lambda b, i: (0, 0)),
        ],
        out_specs=pl.BlockSpec((None, rows, W_B), lambda b, i: (b, i, 0)),
        out_shape=jax.ShapeDtypeStruct((batch, seq, W_B), BF16),
        scratch_shapes=[pltpu.VMEM((B_STREAMS, B_TQ, W_B), F32), pltpu.VMEM((B_STREAMS, B_TQ, B_CAT), F32)],
        compiler_params=pltpu.CompilerParams(
            dimension_semantics=("arbitrary", "arbitrary"), vmem_limit_bytes=VMEM_LIMIT),
        name="mixer_b",
    )(qb.reshape(batch, seq, W_B), kb.reshape(batch, seq, W_B), vb.reshape(batch, seq, W_B), tri)
    return out.reshape(t, W_B)


FFN_TM = 512
FFN_TF = 256
HALO = 16


def _merge_ffn_kernel(x_ref, oa_ref, ob_ref, pg_ref, bg_ref, wbr_ref, wo_ref, g_ref, wup_ref, cw_ref, cb_ref,
                      wd_ref, gf_ref, o_ref, x1_s, h_s, up_s, val_s, acc_s, *, tiles_per_seq, final_norm):
    i = pl.program_id(0)

    gates = jax.nn.sigmoid(pg_ref[...].astype(F32) + bg_ref[...])
    ya = _dot(oa_ref[...], wbr_ref[0:OUT_A, :])
    yb = _dot(ob_ref[...], wbr_ref[OUT_A:OUT_A + W_B, :])
    merged = gates[:, 0:D_MODEL] * ya + gates[:, D_MODEL:2 * D_MODEL] * yb
    x1 = x_ref[...] + _dot(merged.astype(BF16), wo_ref[...])
    x1_s[...] = x1

    @pl.when(i == 0)
    def _():
        h_s[FFN_TM:FFN_TM + HALO, :] = jnp.zeros((HALO, D_MODEL), BF16)

    h_s[0:HALO, :] = h_s[FFN_TM:FFN_TM + HALO, :]
    h_s[HALO:HALO + FFN_TM, :] = _rms(x1, g_ref[...]).astype(BF16)
    seq_start = (i % tiles_per_seq) == 0
    n_chunks = D_FF // FFN_TF

    def project(c):
        slot = c % 2
        up = _dot(h_s[...], wup_ref[:, c * FFN_TF:(c + 1) * FFN_TF])
        for ls in range(FFN_TF // LANES):
            lanes = slice(ls * LANES, (ls + 1) * LANES)
            up_s[slot, ls, pl.ds(0, HALO, stride=2), :] = jnp.where(seq_start, 0.0, up[0:HALO, lanes])
            up_s[slot, ls, pl.ds(2 * HALO, FFN_TM, stride=2), :] = up[HALO:HALO + FFN_TM, lanes]
        val_s[slot] = _dot(h_s[HALO:HALO + FFN_TM, :], wup_ref[:, D_FF + c * FFN_TF:D_FF + (c + 1) * FFN_TF])

    project(0)
    for c in range(n_chunks):
        if c + 1 < n_chunks:
            project(c + 1)
        slot = c % 2
        cols = slice(c * FFN_TF, (c + 1) * FFN_TF)
        parts = []
        for ls in range(FFN_TF // LANES):
            lanes = slice(c * FFN_TF + ls * LANES, c * FFN_TF + (ls + 1) * LANES)
            a = cb_ref[:, lanes]
            for tap in range(CONV_WIDTH):
                lag = CONV_WIDTH - 1 - tap
                a = a + up_s[slot, ls, pl.ds(2 * (HALO - lag), FFN_TM, stride=2), :] * cw_ref[tap:tap + 1, lanes]
            parts.append(a)
        a = jnp.concatenate(parts, axis=1)
        gelu = 0.5 * a * (1.0 + lax.erf(a * (2.0 ** -0.5)))
        down = _dot((gelu * val_s[slot]).astype(BF16), wd_ref[cols, :])
        if c == 0:
            acc_s[...] = down
        else:
            acc_s[...] += down

    y = x1_s[...] + acc_s[...]
    if final_norm:
        y = _rms(y, gf_ref[...])
    o_ref[...] = y


def _merge_ffn(x, oa, ob, pg, bg, wbr, wo, g, w_up, cw, cb, w_down, gf, layer, seq, final_norm):
    t = x.shape[0]
    kern = functools.partial(_merge_ffn_kernel, tiles_per_seq=seq // FFN_TM, final_norm=final_norm)
    row = lambda w: pl.BlockSpec((FFN_TM, w), lambda i: (i, 0))
    resident = lambda r, c: _layer_spec(layer, r, c, pipeline_mode=pl.Buffered(1))
    return pl.pallas_call(
        kern,
        grid=(t // FFN_TM,),
        in_specs=[
            row(D_MODEL), row(OUT_A), row(W_B), row(2 * D_MODEL),
            resident(1, 2 * D_MODEL),
            resident(OUT_A + W_B, D_MODEL),
            resident(D_MODEL, D_MODEL),
            resident(1, D_MODEL),
            resident(D_MODEL, 2 * D_FF),
            resident(CONV_WIDTH, D_FF),
            resident(1, D_FF),
            resident(D_FF, D_MODEL),
            pl.BlockSpec((1, D_MODEL), lambda i: (0, 0)),
        ],
        out_specs=row(D_MODEL),
        out_shape=jax.ShapeDtypeStruct((t, D_MODEL), F32),
        scratch_shapes=[
            pltpu.VMEM((FFN_TM, D_MODEL), F32),
            pltpu.VMEM((HALO + FFN_TM, D_MODEL), BF16),
            pltpu.VMEM((2, FFN_TF // LANES, 2 * (HALO + FFN_TM), LANES), F32),
            pltpu.VMEM((2, FFN_TM, FFN_TF), F32),
            pltpu.VMEM((FFN_TM, D_MODEL), F32),
        ],
        compiler_params=pltpu.CompilerParams(dimension_semantics=("arbitrary",), vmem_limit_bytes=VMEM_LIMIT),
        name="merge_ffn",
    )(x, oa, ob, pg, bg.reshape(-1, 1, 2 * D_MODEL), wbr, wo, g.reshape(-1, 1, D_MODEL), w_up, cw,
      cb.reshape(-1, 1, D_FF), w_down, gf.reshape(1, D_MODEL))


def kernel(x, norm1, w_in, b_gate, w_br, w_o, norm2, w_up, conv_w, conv_b, w_down, norm_f):
    batch, seq, d = x.shape
    depth = norm1.shape[0]
    assert d == D_MODEL and seq % A_TILE == 0 and seq % FFN_TM == 0
    t = batch * seq
    bias = jnp.asarray(_alibi_bias())
    tri = jnp.asarray(_suffix_sum_matrix(), dtype=BF16)
    xf = x.reshape(t, d)
    w_in, w_br, w_o, w_up, w_down = (w.astype(BF16) for w in (w_in, w_br, w_o, w_up, w_down))
    for l in range(depth):
        pa, qb, kb, vb, pg = _in_proj(xf, norm1, w_in, l)
        oa = _mixer_a(pa, bias, batch, seq)
        ob = _mixer_b(qb, kb, vb, tri, batch, seq)
        xf = _merge_ffn(xf, oa, ob, pg, b_gate, w_br, w_o, norm2, w_up, conv_w, conv_b, w_down, norm_f, l, seq,
                        final_norm=(l == depth - 1))
    return xf.reshape(batch, seq, d)
```

```python
import functools

import numpy as np
import jax
import jax.numpy as jnp
from jax import lax
from jax.experimental import pallas as pl
from jax.experimental.pallas import tpu as pltpu

D_MODEL = 1024
HEAD_DIM = 64
DSW_GROUPS = ((128, 1), (512, 4), (2048, 16))
N_GROUPS = len(DSW_GROUPS)
HEADS_PER_GROUP = 4
DSW_HEADS = HEADS_PER_GROUP * N_GROUPS
SB_HEADS = 4
W_A = DSW_HEADS * HEAD_DIM
W_B = SB_HEADS * HEAD_DIM
OUT_A = HEADS_PER_GROUP * HEAD_DIM
N_IN = 3 * W_A + 3 * W_B + 2 * D_MODEL
D_FF = 2816
CONV_WIDTH = 3
RMS_EPS = 1e-6
QK_SCALE = HEAD_DIM ** -0.5

LANES = 128
WIN = 128
MASKED = -1e30

VMEM_LIMIT = 56 * 1024 * 1024

F32 = jnp.float32
BF16 = jnp.bfloat16


def _dot(a, b):
    return jnp.dot(a, b, preferred_element_type=F32)


def _dot_nt(a, b):
    return lax.dot_general(a, b, (((1,), (1,)), ((), ())), preferred_element_type=F32)


def _rms(x, g):
    ms = jnp.mean(x * x, axis=-1, keepdims=True)
    return x * lax.rsqrt(ms + RMS_EPS) * g


IN_TM = 512
IN_CHUNK = 256
B_BLK = 128
B_TQ = 256
B_STREAMS = 4
B_CAT = SB_HEADS * B_BLK
LOG2E = 1.4426950408889634
RUN_FLOOR = -160.0


def _in_proj_kernel(x0_ref, xn_ref, g_ref, w_ref, bg_ref, pa_ref, qb_ref, kb_ref, vb_ref, pg_ref, h_s):
    @pl.when(pl.program_id(0) == 0)
    def _():
        h_s[...] = _rms(x0_ref[...], g_ref[...]).astype(BF16)

    h = h_s[...]
    h_next = _rms(xn_ref[...], g_ref[...]).astype(BF16)

    def proj(base):
        return _dot(h, w_ref[:, base:base + IN_CHUNK])

    for c in range(0, 3 * W_A, IN_CHUNK):
        pa_ref[:, c:c + IN_CHUNK] = proj(c)
    base_b = 3 * W_A
    qb_ref[...] = (proj(base_b) * (-QK_SCALE * LOG2E)).astype(BF16)
    kb_ref[...] = proj(base_b + W_B).astype(BF16)
    vb_ref[...] = proj(base_b + 2 * W_B).astype(BF16)
    base_g = 3 * W_A + 3 * W_B
    for c in range(0, 2 * D_MODEL, IN_CHUNK):
        pg_ref[:, c:c + IN_CHUNK] = proj(base_g + c) + bg_ref[:, c:c + IN_CHUNK]
    h_s[...] = h_next


def _layer_spec(layer, rows, cols, **kw):
    return pl.BlockSpec((None, rows, cols), lambda i: (layer, 0, 0), **kw)


def _in_proj(x, g, w, bg, layer):
    t = x.shape[0]
    last = t // IN_TM - 1
    return pl.pallas_call(
        _in_proj_kernel,
        grid=(t // IN_TM,),
        in_specs=[
            pl.BlockSpec((IN_TM, D_MODEL), lambda i: (0, 0), pipeline_mode=pl.Buffered(1)),
            pl.BlockSpec((IN_TM, D_MODEL), lambda i: (jnp.minimum(i + 1, last), 0)),
            _layer_spec(layer, 1, D_MODEL),
            _layer_spec(layer, D_MODEL, N_IN),
            _layer_spec(layer, 1, 2 * D_MODEL),
        ],
        out_specs=[
            pl.BlockSpec((IN_TM, 3 * W_A), lambda i: (i, 0)),
            pl.BlockSpec((IN_TM, W_B), lambda i: (i, 0)),
            pl.BlockSpec((IN_TM, W_B), lambda i: (i, 0)),
            pl.BlockSpec((IN_TM, W_B), lambda i: (i, 0)),
            pl.BlockSpec((IN_TM, 2 * D_MODEL), lambda i: (i, 0)),
        ],
        out_shape=[
            jax.ShapeDtypeStruct((t, 3 * W_A), F32),
            jax.ShapeDtypeStruct((t, W_B), BF16),
            jax.ShapeDtypeStruct((t, W_B), BF16),
            jax.ShapeDtypeStruct((t, W_B), BF16),
            jax.ShapeDtypeStruct((t, 2 * D_MODEL), F32),
        ],
        scratch_shapes=[pltpu.VMEM((IN_TM, D_MODEL), BF16)],
        compiler_params=pltpu.CompilerParams(dimension_semantics=("arbitrary",), vmem_limit_bytes=VMEM_LIMIT),
        name="in_proj",
    )(x, x, g.reshape(-1, 1, D_MODEL), w, bg.reshape(-1, 1, 2 * D_MODEL))


A_TILE = WIN * max(d for _, d in DSW_GROUPS)


def _alibi_bias():
    slopes = 2.0 ** (-8.0 * np.arange(1, DSW_HEADS + 1) / DSW_HEADS)
    qi = np.arange(WIN)[:, None]
    kj = np.arange(2 * WIN)[None, :]
    delta = qi + WIN - kj
    valid = (delta >= 0) & (delta <= WIN)
    out = np.empty((N_GROUPS, 2, 2, 2 * WIN, 2 * WIN), np.float32)
    for g, (_, dil) in enumerate(DSW_GROUPS):
        for h in range(HEADS_PER_GROUP):
            b = -slopes[g * HEADS_PER_GROUP + h] * LOG2E * (delta * dil).astype(np.float32)
            rows = slice((h % 2) * WIN, (h % 2 + 1) * WIN)
            out[g, h // 2, 0, rows] = np.where(valid, b, MASKED)
            out[g, h // 2, 1, rows] = np.where(valid & (kj >= WIN), b, MASKED)
    return out


A_MAX_STRIDE = 4
A_UNROLL = {1: 15, 4: 12, 16: 16}


def _mixer_a_kernel(*refs):
    q_refs = refs[0:3]
    kc_refs = refs[3:6]
    vc_refs = refs[6:9]
    kp_refs = refs[9:12]
    vp_refs = refs[12:15]
    bias_ref = refs[15]
    o_ref = refs[16]
    acc_s, m_s, l_s, stage_s = refs[17:21]

    first = (pl.program_id(1) == 0).astype(jnp.int32)
    lane = lax.broadcasted_iota(jnp.int32, (WIN, LANES), 1)
    head0 = lane < HEAD_DIM
    ones = jnp.ones((2 * WIN, LANES), BF16)
    n_units = A_TILE // WIN

    for g, (_, dil) in enumerate(DSW_GROUPS):
        q_ref, kc_ref, vc_ref = q_refs[g], kc_refs[g], vc_refs[g]

        staged = {}
        span = A_TILE // A_MAX_STRIDE
        if dil > A_MAX_STRIDE:
            for slot, ref in enumerate((q_ref, kc_ref, vc_ref, kp_refs[g], vp_refs[g])):
                for r in range(A_MAX_STRIDE):
                    stage_s[slot, r * span:(r + 1) * span, :] = ref[pl.ds(r, span, stride=A_MAX_STRIDE), :]
                staged[id(ref)] = slot

        def rows(ref, start, dil=dil, staged=staged, span=span):
            if dil == 1:
                return ref[pl.ds(start, WIN), :]
            if id(ref) in staged:
                first_row = (start % A_MAX_STRIDE) * span + start // A_MAX_STRIDE
                return stage_s[staged[id(ref)], pl.ds(first_row, WIN, stride=dil // A_MAX_STRIDE), :]
            return ref[pl.ds(start, WIN, stride=dil), :]

        def unit(kprev_ref, vprev_ref, prev_off, off, sel, g=g, dil=dil, q_ref=q_ref, kc_ref=kc_ref,
                 vc_ref=vc_ref, rows=rows):
            q = rows(q_ref, off) * (QK_SCALE * LOG2E)
            q2 = jnp.concatenate([jnp.where(head0, q, 0.0), jnp.where(head0, 0.0, q)], axis=0).astype(BF16)
            k = jnp.concatenate([rows(kprev_ref, prev_off), rows(kc_ref, off)], axis=0).astype(BF16)
            v = jnp.concatenate([rows(vprev_ref, prev_off), rows(vc_ref, off)], axis=0).astype(BF16)
            s = _dot_nt(q2, k) + bias_ref[g, sel]
            m = jnp.max(s, axis=-1, keepdims=True)
            p = jnp.exp2(s - m).astype(BF16)
            pvl = _dot(p, jnp.concatenate([v, ones], axis=1))
            dst = pl.ds(off, WIN) if dil == 1 else pl.ds(off, WIN, stride=dil)
            acc_s[g, dst, :] = jnp.where(head0, pvl[0:WIN, 0:LANES], pvl[WIN:2 * WIN, 0:LANES])
            l_s[g, dst, :] = jnp.where(head0, pvl[0:WIN, LANES:2 * LANES], pvl[WIN:2 * WIN, LANES:2 * LANES])
            m_s[g, dst, :] = jnp.where(head0, m[0:WIN], m[WIN:2 * WIN])

        def head_unit(r, unit=unit, kp_ref=kp_refs[g], vp_ref=vp_refs[g]):
            unit(kp_ref, vp_ref, r, r, first)

        def tail_unit(idx, unit=unit, kc_ref=kc_ref, vc_ref=vc_ref, dil=dil):
            shift = dil.bit_length() - 1
            u = idx >> shift if isinstance(idx, int) else lax.shift_right_logical(idx, shift)
            off = u * (WIN * dil) + (idx & (dil - 1))
            unit(kc_ref, vc_ref, off - WIN * dil, off, 0)

        unroll = A_UNROLL[dil]
        for fn, lo, hi in ((head_unit, 0, dil), (tail_unit, dil, n_units)):
            count = hi - lo
            if count == 0:
                continue
            if count <= unroll:
                for idx in range(lo, hi):
                    fn(idx)
                continue
            assert count % unroll == 0

            def body(t, c, fn=fn, lo=lo, unroll=unroll):
                for uu in range(unroll):
                    fn(lo + t * unroll + uu)
                return c

            lax.fori_loop(0, count // unroll, body, 0)

    m_all = jnp.maximum(jnp.maximum(m_s[0], m_s[1]), m_s[2])
    num = jnp.zeros((A_TILE, LANES), F32)
    den = jnp.zeros((A_TILE, LANES), F32)
    for g in range(N_GROUPS):
        w = jnp.exp2(m_s[g] - m_all)
        num = num + w * acc_s[g]
        den = den + w * l_s[g]
    o_ref[...] = (num / den).astype(o_ref.dtype)


def _mixer_a(pa, bias, batch, seq):
    t = pa.shape[0]
    tiles = seq // A_TILE
    qcol = lambda g: (lambda b, n, hp, g=g: (b * tiles + n, 0 * 6 + 2 * g + hp))
    kcol = lambda g: (lambda b, n, hp, g=g: (b * tiles + n, 1 * 6 + 2 * g + hp))
    vcol = lambda g: (lambda b, n, hp, g=g: (b * tiles + n, 2 * 6 + 2 * g + hp))

    def prev(kind, g):
        per_tile = A_TILE // (WIN * DSW_GROUPS[g][1])
        return lambda b, n, hp: (jnp.maximum((b * tiles + n) * per_tile - 1, 0), kind * 6 + 2 * g + hp)

    in_specs = (
        [pl.BlockSpec((A_TILE, LANES), qcol(g)) for g in range(N_GROUPS)]
        + [pl.BlockSpec((A_TILE, LANES), kcol(g)) for g in range(N_GROUPS)]
        + [pl.BlockSpec((A_TILE, LANES), vcol(g)) for g in range(N_GROUPS)]
        + [pl.BlockSpec((WIN * DSW_GROUPS[g][1], LANES), prev(1, g)) for g in range(N_GROUPS)]
        + [pl.BlockSpec((WIN * DSW_GROUPS[g][1], LANES), prev(2, g)) for g in range(N_GROUPS)]
        + [pl.BlockSpec((N_GROUPS, None, 2, 2 * WIN, 2 * WIN), lambda b, n, hp: (0, hp, 0, 0, 0))]
    )
    return pl.pallas_call(
        _mixer_a_kernel,
        grid=(batch, tiles, 2),
        in_specs=in_specs,
        out_specs=pl.BlockSpec((A_TILE, LANES), lambda b, n, hp: (b * tiles + n, hp)),
        out_shape=jax.ShapeDtypeStruct((t, OUT_A), BF16),
        scratch_shapes=[pltpu.VMEM((N_GROUPS, A_TILE, LANES), F32)] * 3 + [pltpu.VMEM((5, A_TILE, LANES), F32)],
        compiler_params=pltpu.CompilerParams(
            dimension_semantics=("arbitrary", "arbitrary", "arbitrary"), vmem_limit_bytes=VMEM_LIMIT),
        name="mixer_a",
    )(*([pa] * 15), bias)


def _suffix_sum_matrix():
    j = np.arange(B_BLK)[:, None]
    s = np.arange(B_BLK)[None, :]
    later = (j > s).astype(np.float32)
    ones = np.ones((B_BLK, B_BLK), np.float32)
    zero = np.zeros((B_BLK, B_BLK), np.float32)
    return np.block([[later, zero, ones, zero], [zero, later, zero, ones]])


def _mixer_b_kernel(q_ref, k_ref, v_ref, tri_ref, o_ref, acc_s, run_s):
    step = pl.program_id(1)
    tri = tri_ref[...]
    t_loc = lax.broadcasted_iota(jnp.int32, (B_TQ, B_CAT), 0)
    s_loc = jnp.bitwise_and(lax.broadcasted_iota(jnp.int32, (B_TQ, B_CAT), 1), B_BLK - 1)
    per_q = B_TQ // B_BLK
    lane = lax.broadcasted_iota(jnp.int32, (B_BLK, W_B), 1)
    head_lanes = [(lane >= h * HEAD_DIM) & (lane < (h + 1) * HEAD_DIM) for h in range(SB_HEADS)]

    def expand(ref, j):
        blk = ref[pl.ds(pl.multiple_of(j * B_BLK, B_BLK), B_BLK), :]
        return jnp.concatenate([jnp.where(keep, blk, jnp.zeros_like(blk)) for keep in head_lanes], axis=0)

    def block(s, kv, diag_offset, first):
        zn = _dot_nt(q_ref[s * B_TQ:(s + 1) * B_TQ, :], kv[0])
        soft = jnp.log2(1.0 + jnp.exp2(-jnp.abs(zn)))
        log_stay = jnp.minimum(zn, 0.0) - soft
        if diag_offset is not None:
            causal = s_loc + diag_offset < t_loc
            log_stay = jnp.where(causal, log_stay, 0.0)
        ls16 = log_stay.astype(BF16)
        half = SB_HEADS // 2 * B_BLK
        sums = [_dot(ls16[:, p * half:(p + 1) * half], tri) for p in range(2)]
        later = jnp.concatenate([x[:, 0:half] for x in sums], axis=1)
        whole = jnp.concatenate([x[:, half:2 * half] for x in sums], axis=1)
        expo = (log_stay - zn) + later
        if not first:
            run = run_s[s]
            expo = expo + run
            whole = whole + run
        p = jnp.exp2(expo)
        if diag_offset is not None:
            p = jnp.where(causal, p, 0.0)
        run_s[s] = whole
        pv = _dot(p.astype(BF16), kv[1])
        if first:
            acc_s[s] = pv
        else:
            acc_s[s] += pv

    def straight_line(streams_with_trip):
        base = step * (B_STREAMS * per_q)
        cache = {}

        def kv(rel):
            if rel not in cache:
                cache[rel] = (expand(k_ref, base + rel), expand(v_ref, base + rel))
            return cache[rel]

        for s in range(B_STREAMS):
            for d in range(per_q - 1, -1, -1):
                block(s, kv(s * per_q + d), d * B_BLK, first=(d == per_q - 1))
            if s in streams_with_trip:
                for d in range(per_q):
                    block(s, kv(s * per_q - 1 - d), None, first=False)

    def trip(s, jj):
        i = step * B_STREAMS + s
        for d in range(per_q):
            j = (i - jj) * per_q - 1 - d
            block(s, (expand(k_ref, j), expand(v_ref, j)), None, first=False)

    @pl.when(step > 0)
    def _():
        straight_line(range(B_STREAMS))

    @pl.when(step == 0)
    def _():
        straight_line(range(1, B_STREAMS))

    def alive(s):
        run = run_s[s]
        top = run[:, 0:B_BLK]
        for h in range(1, SB_HEADS):
            top = jnp.maximum(top, run[:, h * B_BLK:(h + 1) * B_BLK])
        return jnp.max(top) > RUN_FLOOR

    live0 = [alive(s) for s in range(B_STREAMS)]
    for s in range(B_STREAMS):
        i = step * B_STREAMS + s

        def cond(c, i=i):
            jj, live = c
            return jnp.logical_and(jj < i, live)

        def body(c, s=s):
            jj, _ = c
            trip(s, jj)
            return jj + 1, alive(s)

        lax.while_loop(cond, body, (jnp.int32(1), live0[s]))

    o_ref[...] = acc_s[...].reshape(B_STREAMS * B_TQ, W_B).astype(o_ref.dtype)


def _mixer_b(qb, kb, vb, tri, batch, seq):
    t = qb.shape[0]
    rows = B_STREAMS * B_TQ
    whole_seq = pl.BlockSpec((None, seq, W_B), lambda b, i: (b, 0, 0))
    out = pl.pallas_call(
        _mixer_b_kernel,
        grid=(batch, seq // rows),
        in_specs=[
            pl.BlockSpec((None, rows, W_B), lambda b, i: (b, i, 0)),
            whole_seq,
            whole_seq,
            pl.BlockSpec((2 * B_BLK, 4 * B_BLK), lambda b, i: (0, 0)),
        ],
        out_specs=pl.BlockSpec((None, rows, W_B), lambda b, i: (b, i, 0)),
        out_shape=jax.ShapeDtypeStruct((batch, seq, W_B), BF16),
        scratch_shapes=[pltpu.VMEM((B_STREAMS, B_TQ, W_B), F32), pltpu.VMEM((B_STREAMS, B_TQ, B_CAT), F32)],
        compiler_params=pltpu.CompilerParams(
            dimension_semantics=("arbitrary", "arbitrary"), vmem_limit_bytes=VMEM_LIMIT),
        name="mixer_b",
    )(qb.reshape(batch, seq, W_B), kb.reshape(batch, seq, W_B), vb.reshape(batch, seq, W_B), tri)
    return out.reshape(t, W_B)


FFN_TM = 512
FFN_TF = 256
HALO = 16


def _merge_ffn_kernel(x_ref, oa_ref, ob_ref, pg_ref, wbr_ref, wo_ref, g_ref, wup_ref, cw_ref, cb_ref,
                      wd_ref, gf_ref, o_ref, x1_s, h_s, up_s, val_s, acc_s, *, tiles_per_seq, final_norm):
    i = pl.program_id(0)

    gates = jax.nn.sigmoid(pg_ref[...])
    ya = _dot(oa_ref[...], wbr_ref[0:OUT_A, :])
    yb = _dot(ob_ref[...], wbr_ref[OUT_A:OUT_A + W_B, :])
    merged = gates[:, 0:D_MODEL] * ya + gates[:, D_MODEL:2 * D_MODEL] * yb
    x1 = x_ref[...] + _dot(merged.astype(BF16), wo_ref[...])
    x1_s[...] = x1

    @pl.when(i == 0)
    def _():
        h_s[FFN_TM:FFN_TM + HALO, :] = jnp.zeros((HALO, D_MODEL), BF16)

    h_s[0:HALO, :] = h_s[FFN_TM:FFN_TM + HALO, :]
    h_s[HALO:HALO + FFN_TM, :] = _rms(x1, g_ref[...]).astype(BF16)
    seq_start = (i % tiles_per_seq) == 0
    n_chunks = D_FF // FFN_TF

    def project(c):
        slot = c % 2
        up = _dot(h_s[...], wup_ref[:, c * FFN_TF:(c + 1) * FFN_TF])
        for ls in range(FFN_TF // LANES):
            lanes = slice(ls * LANES, (ls + 1) * LANES)
            up_s[slot, ls, pl.ds(0, HALO, stride=2), :] = jnp.where(seq_start, 0.0, up[0:HALO, lanes])
            up_s[slot, ls, pl.ds(2 * HALO, FFN_TM, stride=2), :] = up[HALO:HALO + FFN_TM, lanes]
        val_s[slot] = _dot(h_s[HALO:HALO + FFN_TM, :], wup_ref[:, D_FF + c * FFN_TF:D_FF + (c + 1) * FFN_TF])

    project(0)
    for c in range(n_chunks):
        if c + 1 < n_chunks:
            project(c + 1)
        slot = c % 2
        cols = slice(c * FFN_TF, (c + 1) * FFN_TF)
        parts = []
        for ls in range(FFN_TF // LANES):
            lanes = slice(c * FFN_TF + ls * LANES, c * FFN_TF + (ls + 1) * LANES)
            a = cb_ref[:, lanes]
            for tap in range(CONV_WIDTH):
                lag = CONV_WIDTH - 1 - tap
                a = a + up_s[slot, ls, pl.ds(2 * (HALO - lag), FFN_TM, stride=2), :] * cw_ref[tap:tap + 1, lanes]
            parts.append(a)
        a = jnp.concatenate(parts, axis=1)
        gelu = 0.5 * a * (1.0 + lax.erf(a * (2.0 ** -0.5)))
        down = _dot((gelu * val_s[slot]).astype(BF16), wd_ref[cols, :])
        if c == 0:
            acc_s[...] = down
        else:
            acc_s[...] += down

    y = x1_s[...] + acc_s[...]
    if final_norm:
        y = _rms(y, gf_ref[...])
    o_ref[...] = y


def _merge_ffn(x, oa, ob, pg, wbr, wo, g, w_up, cw, cb, w_down, gf, layer, seq, final_norm):
    t = x.shape[0]
    kern = functools.partial(_merge_ffn_kernel, tiles_per_seq=seq // FFN_TM, final_norm=final_norm)
    row = lambda w: pl.BlockSpec((FFN_TM, w), lambda i: (i, 0))
    resident = lambda r, c: _layer_spec(layer, r, c, pipeline_mode=pl.Buffered(1))
    return pl.pallas_call(
        kern,
        grid=(t // FFN_TM,),
        in_specs=[
            row(D_MODEL), row(OUT_A), row(W_B), row(2 * D_MODEL),
            resident(OUT_A + W_B, D_MODEL),
            resident(D_MODEL, D_MODEL),
            resident(1, D_MODEL),
            resident(D_MODEL, 2 * D_FF),
            resident(CONV_WIDTH, D_FF),
            resident(1, D_FF),
            resident(D_FF, D_MODEL),
            pl.BlockSpec((1, D_MODEL), lambda i: (0, 0)),
        ],
        out_specs=row(D_MODEL),
        out_shape=jax.ShapeDtypeStruct((t, D_MODEL), F32),
        scratch_shapes=[
            pltpu.VMEM((FFN_TM, D_MODEL), F32),
            pltpu.VMEM((HALO + FFN_TM, D_MODEL), BF16),
            pltpu.VMEM((2, FFN_TF // LANES, 2 * (HALO + FFN_TM), LANES), F32),
            pltpu.VMEM((2, FFN_TM, FFN_TF), F32),
            pltpu.VMEM((FFN_TM, D_MODEL), F32),
        ],
        compiler_params=pltpu.CompilerParams(dimension_semantics=("arbitrary",), vmem_limit_bytes=VMEM_LIMIT),
        name="merge_ffn",
    )(x, oa, ob, pg, wbr, wo, g.reshape(-1, 1, D_MODEL), w_up, cw,
      cb.reshape(-1, 1, D_FF), w_down, gf.reshape(1, D_MODEL))


def kernel(x, norm1, w_in, b_gate, w_br, w_o, norm2, w_up, conv_w, conv_b, w_down, norm_f):
    batch, seq, d = x.shape
    depth = norm1.shape[0]
    assert d == D_MODEL and seq % A_TILE == 0 and seq % FFN_TM == 0
    t = batch * seq
    bias = jnp.asarray(_alibi_bias())
    tri = jnp.asarray(_suffix_sum_matrix(), dtype=BF16)
    xf = x.reshape(t, d)
    w_in, w_br, w_o, w_up, w_down = (w.astype(BF16) for w in (w_in, w_br, w_o, w_up, w_down))
    for l in range(depth):
        pa, qb, kb, vb, pg = _in_proj(xf, norm1, w_in, b_gate, l)
        oa = _mixer_a(pa, bias, batch, seq)
        ob = _mixer_b(qb, kb, vb, tri, batch, seq)
        xf = _merge_ffn(xf, oa, ob, pg, w_br, w_o, norm2, w_up, conv_w, conv_b, w_down, norm_f, l, seq,
                        final_norm=(l == depth - 1))
    return xf.reshape(batch, seq, d)
```
